```python
import jax, jax.numpy as jnp
from jax import lax
import numpy as np

D_MODEL = 1024
BATCH = 8
SEQ = 2048
DEPTH = 1
DEC_BATCH = 128
DEC_SEQ = 4
PAST_LEN = 16384
PAGE_SIZE = 128

D_RNN = D_MODEL
N_GATE_BLOCKS = 8
GATE_BLOCK = D_RNN // N_GATE_BLOCKS
CONV_A_WIDTH = 4
RG_C = 8.0
D_CONF = D_MODEL
CONF_WIDTH = 31
N_EXPERTS = 32
TOP_K = 4
D_EXPERT = D_MODEL
SWIGLU_LIMIT = 7.0
SWIGLU_ALPHA = 1.702
EPS = 1e-6
D_IN = D_RNN + 2 * D_CONF + 2 * D_MODEL

kernel_name = 'hawk_conformer_moe_parallel_step'


def rmsnorm(x, g):
    x32 = x.astype(jnp.float32)
    y = x32 * lax.rsqrt(jnp.mean(x32 * x32, axis=-1, keepdims=True) + EPS)
    return (y * g.astype(jnp.float32)).astype(x.dtype)


def layernorm(x, g, b):
    x32 = x.astype(jnp.float32)
    mu = jnp.mean(x32, axis=-1, keepdims=True)
    xc = x32 - mu
    y = xc * lax.rsqrt(jnp.mean(xc * xc, axis=-1, keepdims=True) + EPS)
    return (y * g.astype(jnp.float32) + b.astype(jnp.float32)).astype(x.dtype)


def causal_dwconv(x, buf, w, b):
    k = w.shape[0]
    xcat = jnp.concatenate([buf.astype(x.dtype), x], axis=1)
    y = lax.conv_general_dilated(xcat, w[:, None, :].astype(x.dtype), window_strides=(1,),
                                 padding='VALID', dimension_numbers=('NWC', 'WIO', 'NWC'),
                                 feature_group_count=x.shape[-1])
    return y + b, xcat[:, xcat.shape[1] - (k - 1):]


def block_diag(x, w, b):
    bsz, t, _ = x.shape
    xb = x.reshape(bsz, t, N_GATE_BLOCKS, GATE_BLOCK)
    y = jnp.einsum('btnc,ncd->btnd', xb, w)
    return y.reshape(bsz, t, D_RNN) + b


def rglru(x, h0, w_a, b_a, w_x, b_x, lam):
    r = jax.nn.sigmoid(block_diag(x, w_a, b_a).astype(jnp.float32))
    i = jax.nn.sigmoid(block_diag(x, w_x, b_x).astype(jnp.float32))
    log_a = -RG_C * r * jax.nn.softplus(-lam.astype(jnp.float32))
    a = jnp.exp(log_a)
    u = jnp.sqrt(-jnp.expm1(2.0 * log_a)) * (i * x.astype(jnp.float32))

    def step(h, au):
        a_t, u_t = au
        h = a_t * h + u_t
        return h, h

    h_last, hs = lax.scan(step, h0.astype(jnp.float32),
                          (jnp.swapaxes(a, 0, 1), jnp.swapaxes(u, 0, 1)))
    return jnp.swapaxes(hs, 0, 1).astype(x.dtype), h_last


def moe(x, w_router, b_router, w_gate, b_gate, w_up, b_up, w_down, b_down):
    bsz, t, d = x.shape
    xf = x.reshape(-1, d)
    logits = (xf @ w_router).astype(jnp.float32) + b_router.astype(jnp.float32)
    vals, idx = lax.top_k(logits, TOP_K)
    probs = jax.nn.softmax(vals, axis=-1)
    combine = jnp.einsum('nk,nke->ne', probs, jax.nn.one_hot(idx, N_EXPERTS, dtype=jnp.float32))
    out = jnp.zeros((xf.shape[0], d), jnp.float32)
    for e in range(N_EXPERTS):
        g = jnp.minimum(xf @ w_gate[e] + b_gate[e], SWIGLU_LIMIT)
        u = jnp.clip(xf @ w_up[e] + b_up[e], -SWIGLU_LIMIT, SWIGLU_LIMIT)
        hdn = (u + 1.0) * (g * jax.nn.sigmoid(SWIGLU_ALPHA * g))
        out = out + combine[:, e:e + 1] * (hdn @ w_down[e] + b_down[e]).astype(jnp.float32)
    return out.astype(x.dtype).reshape(bsz, t, d)


def layer(x, conv_a_buf, h0, conf_buf, g_mix, w_in, w_conv_a, b_conv_a, w_rg_a, b_rg_a,
          w_rg_x, b_rg_x, rg_lambda, w_conf_dw, b_conf_dw, g_conf_ln, b_conf_ln, w_conf_pw,
          b_conf_pw, w_out, g_ffn, w_router, b_router, w_gate, b_gate, w_up, b_up, w_down, b_down):
    hn = rmsnorm(x, g_mix)
    z = hn @ w_in
    o1 = D_RNN
    o2 = o1 + 2 * D_CONF
    o3 = o2 + D_MODEL
    xa, u_glu, gate_a, gate_b = z[..., :o1], z[..., o1:o2], z[..., o2:o3], z[..., o3:]
    xa_c, new_conv_a = causal_dwconv(xa, conv_a_buf, w_conv_a, b_conv_a)
    y_a, h_last = rglru(xa_c, h0, w_rg_a, b_rg_a, w_rg_x, b_rg_x, rg_lambda)
    v = u_glu[..., :D_CONF] * jax.nn.sigmoid(u_glu[..., D_CONF:])
    v_c, new_conf = causal_dwconv(v, conf_buf, w_conf_dw, b_conf_dw)
    y_b = jax.nn.silu(layernorm(v_c, g_conf_ln, b_conf_ln)) @ w_conf_pw + b_conf_pw
    m = jax.nn.sigmoid(gate_a) * y_a + jax.nn.sigmoid(gate_b) * y_b
    h = x + m @ w_out
    y = h + moe(rmsnorm(h, g_ffn), w_router, b_router, w_gate, b_gate, w_up, b_up, w_down, b_down)
    return y, new_conv_a, h_last, new_conf


def setup_inputs(seed: int = 0) -> dict:
    key = jax.random.key(seed)
    ks = jax.random.split(key, 40)
    f32 = jnp.float32
    nrm = lambda k, shape, s: jax.random.normal(k, shape, f32) * s
    L = DEPTH
    lam_u = jax.random.uniform(ks[10], (L, D_RNN), f32, minval=0.9, maxval=0.999)
    lam_s = lam_u ** (1.0 / RG_C)
    return {
        'x_prompt': nrm(ks[0], (BATCH, SEQ, D_MODEL), 1.0),
        'x_sample': nrm(ks[1], (DEC_BATCH, DEC_SEQ, D_MODEL), 1.0),
        'state_conv_a': nrm(ks[2], (L, DEC_BATCH, CONV_A_WIDTH - 1, D_RNN), 1.0),
        'state_h': nrm(ks[3], (L, DEC_BATCH, D_RNN), 0.5),
        'state_conf': nrm(ks[4], (L, DEC_BATCH, CONF_WIDTH - 1, D_CONF), 0.5),
        'g_mix': 1.0 + nrm(ks[5], (L, D_MODEL), 0.02),
        'w_in': nrm(ks[6], (L, D_MODEL, D_IN), D_MODEL ** -0.5),
        'w_conv_a': nrm(ks[7], (L, CONV_A_WIDTH, D_RNN), CONV_A_WIDTH ** -0.5),
        'b_conv_a': nrm(ks[8], (L, D_RNN), 0.02),
        'w_rg_a': nrm(ks[9], (L, N_GATE_BLOCKS, GATE_BLOCK, GATE_BLOCK), GATE_BLOCK ** -0.5),
        'b_rg_a': nrm(ks[11], (L, D_RNN), 0.02),
        'w_rg_x': nrm(ks[12], (L, N_GATE_BLOCKS, GATE_BLOCK, GATE_BLOCK), GATE_BLOCK ** -0.5),
        'b_rg_x': nrm(ks[13], (L, D_RNN), 0.02),
        'rg_lambda': jnp.log(lam_s) - jnp.log1p(-lam_s),
        'w_conf_dw': nrm(ks[14], (L, CONF_WIDTH, D_CONF), CONF_WIDTH ** -0.5),
        'b_conf_dw': nrm(ks[15], (L, D_CONF), 0.02),
        'g_conf_ln': 1.0 + nrm(ks[16], (L, D_CONF), 0.02),
        'b_conf_ln': nrm(ks[17], (L, D_CONF), 0.02),
        'w_conf_pw': nrm(ks[18], (L, D_CONF, D_MODEL), D_CONF ** -0.5),
        'b_conf_pw': nrm(ks[19], (L, D_MODEL), 0.02),
        'w_out': nrm(ks[20], (L, D_MODEL, D_MODEL), D_MODEL ** -0.5),
        'g_ffn': 1.0 + nrm(ks[21], (L, D_MODEL), 0.02),
        'w_router': nrm(ks[22], (L, D_MODEL, N_EXPERTS), D_MODEL ** -0.5),
        'b_router': nrm(ks[23], (L, N_EXPERTS), 0.01),
        'w_gate': nrm(ks[24], (L, N_EXPERTS, D_MODEL, D_EXPERT), D_MODEL ** -0.5),
        'b_gate': nrm(ks[25], (L, N_EXPERTS, D_EXPERT), 0.02),
        'w_up': nrm(ks[26], (L, N_EXPERTS, D_MODEL, D_EXPERT), D_MODEL ** -0.5),
        'b_up': nrm(ks[27], (L, N_EXPERTS, D_EXPERT), 0.02),
        'w_down': nrm(ks[28], (L, N_EXPERTS, D_EXPERT, D_MODEL), D_EXPERT ** -0.5),
        'b_down': nrm(ks[29], (L, N_EXPERTS, D_MODEL), 0.02),
        'g_final': 1.0 + nrm(ks[30], (D_MODEL,), 0.02),
    }


def reference(x_prompt, x_sample, state_conv_a, state_h, state_conf, g_mix, w_in, w_conv_a,
              b_conv_a, w_rg_a, b_rg_a, w_rg_x, b_rg_x, rg_lambda, w_conf_dw, b_conf_dw,
              g_conf_ln, b_conf_ln, w_conf_pw, b_conf_pw, w_out, g_ffn, w_router, b_router,
              w_gate, b_gate, w_up, b_up, w_down, b_down, g_final):
    layer_params = (g_mix, w_in, w_conv_a, b_conv_a, w_rg_a, b_rg_a, w_rg_x, b_rg_x, rg_lambda,
                    w_conf_dw, b_conf_dw, g_conf_ln, b_conf_ln, w_conf_pw, b_conf_pw, w_out,
                    g_ffn, w_router, b_router, w_gate, b_gate, w_up, b_up, w_down, b_down)
    bp = x_prompt.shape[0]
    hp, hs = x_prompt, x_sample
    p_conv_a, p_h, p_conf, s_conv_a, s_h, s_conf = [], [], [], [], [], []
    for l in range(DEPTH):
        lp = [p[l] for p in layer_params]
        hp, ca, hh, cf = layer(hp,
                               jnp.zeros((bp, CONV_A_WIDTH - 1, D_RNN), hp.dtype),
                               jnp.zeros((bp, D_RNN), jnp.float32),
                               jnp.zeros((bp, CONF_WIDTH - 1, D_CONF), hp.dtype),
                               *lp)
        p_conv_a.append(ca)
        p_h.append(hh)
        p_conf.append(cf)
        hs, ca, hh, cf = layer(hs, state_conv_a[l], state_h[l], state_conf[l], *lp)
        s_conv_a.append(ca)
        s_h.append(hh)
        s_conf.append(cf)
    y_prompt = rmsnorm(hp, g_final)
    y_sample = rmsnorm(hs, g_final)
    return (y_prompt, y_sample, jnp.stack(p_conv_a), jnp.stack(p_h), jnp.stack(p_conf),
            jnp.stack(s_conv_a), jnp.stack(s_h), jnp.stack(s_conf))
```

```python
import functools

import jax
import jax.numpy as jnp
from jax import lax
from jax.experimental import pallas as pl
from jax.experimental.pallas import tpu as pltpu

EPS = 1e-6
RG_C = 8.0
SWIGLU_LIMIT = 7.0
SWIGLU_ALPHA = 1.702
TOP_K = 4
LANE = 128
SUBLANE = 8
CONV_A_HALO = 8
CONF_HALO = 32

_BF = jnp.bfloat16
_F32 = jnp.float32


def _const_spec(shape):
    nd = len(shape)
    return pl.BlockSpec(shape, lambda *_: (0,) * nd, pipeline_mode=pl.Buffered(1))


def _rms(x, g):
    return x * lax.rsqrt(jnp.mean(x * x, axis=-1, keepdims=True) + EPS) * g


def _sigmoid(x):
    return jax.nn.sigmoid(x)


def _dot(a, b):
    return jnp.dot(a, b, preferred_element_type=_F32)


def _block_diag(xb, w_ref, b_ref, out_ref):
    nb, gb, _ = w_ref.shape
    for n in range(nb):
        cs = slice(n * gb, (n + 1) * gb)
        out_ref[:, cs] = _dot(xb[:, cs], w_ref[n]) + b_ref[:, cs]


def _rglru_coeffs(xc_ref, r_ref, i_ref, lam_ref, a_ref, u_ref):
    sp = jax.nn.softplus(-lam_ref[...])
    r = _sigmoid(r_ref[...])
    i = _sigmoid(i_ref[...])
    log_a = -RG_C * r * sp
    a = jnp.exp(log_a)
    a_ref[...] = a
    u_ref[...] = jnp.sqrt(-jnp.tanh(log_a) * (a * a + 1.0)) * (i * xc_ref[...])


def _layernorm_silu(vc, g, b):
    mu = jnp.mean(vc, axis=-1, keepdims=True)
    xc = vc - mu
    y = xc * lax.rsqrt(jnp.mean(xc * xc, axis=-1, keepdims=True) + EPS) * g + b
    return y * _sigmoid(y)


def _route(xn_b, wrt_ref, br_ref, cnt_ref, idx_ref, prob_ref, rank_ref):
    ne = wrt_ref.shape[0]
    tt = xn_b.shape[0]
    logits = lax.dot_general(wrt_ref[...], xn_b, (((1,), (1,)), ((), ())),
                             preferred_element_type=_F32) + br_ref[...]
    eidx = lax.broadcasted_iota(jnp.int32, (ne, tt), 0).astype(_F32)
    work = logits
    vals, picks, onehots = [], [], []
    for _ in range(TOP_K):
        m = jnp.max(work, axis=0, keepdims=True)
        pick = jnp.min(jnp.where(work == m, eidx, float(ne)), axis=0, keepdims=True)
        one = eidx == pick
        work = jnp.where(one, -jnp.inf, work)
        vals.append(m)
        picks.append(pick)
        onehots.append(one)
    exps = [jnp.exp(v - vals[0]) for v in vals]
    denom = exps[0] + exps[1] + exps[2] + exps[3]
    inv = 1.0 / denom
    sel = jnp.zeros((ne, tt), _F32)
    for one in onehots:
        sel = sel + one.astype(_F32)
    row = lax.broadcasted_iota(jnp.int32, (tt, tt), 0)
    col = lax.broadcasted_iota(jnp.int32, (tt, tt), 1)
    tri = (row <= col).astype(_BF)
    incl = _dot(sel.astype(_BF), tri)
    base = cnt_ref[:, 0:1]
    before = base + (incl - sel)
    for k in range(TOP_K):
        idx_ref[k:k + 1, :] = picks[k].astype(jnp.int32)
        prob_ref[k:k + 1, :] = exps[k] * inv
        rk = jnp.sum(jnp.where(onehots[k], before, 0.0), axis=0, keepdims=True)
        rank_ref[k:k + 1, :] = rk.astype(jnp.int32)
    cnt_ref[...] = cnt_ref[...] + incl[:, tt - 1:tt]


def _tail(x, ya, yb, ga, gb, wout_ref, gffn_ref, h_ref, xn_ref):
    m = _sigmoid(ga) * ya + _sigmoid(gb) * yb
    h = x + _dot(m.astype(_BF), wout_ref[...])
    h_ref[...] = h
    xn = _rms(h, gffn_ref[...])
    xn_ref[...] = xn
    return xn.astype(_BF)


def _causal_conv_tile(x, ext_ref, w_ref, b_ref, out_ref, first):
    tt, d = x.shape
    halo = ext_ref.shape[1] - tt
    kk = w_ref.shape[0]
    off = halo - (kk - 1)
    groups = 4
    rows_per_iter = groups * SUBLANE
    for c in range(d // LANE):
        cs = slice(c * LANE, (c + 1) * LANE)

        @pl.when(first)
        def _():
            ext_ref[c, 0:halo, :] = jnp.zeros((halo, LANE), _F32)

        ext_ref[c, halo:halo + tt, :] = x[:, cs]
        wb = [jnp.broadcast_to(w_ref[k:k + 1, cs], (SUBLANE, LANE)) for k in range(kk)]
        bb = jnp.broadcast_to(b_ref[0:1, cs], (SUBLANE, LANE))

        def blk(i, carry, c=c, cs=cs, wb=wb, bb=bb):
            r0 = pl.multiple_of(i * rows_per_iter, rows_per_iter)
            for q in range(groups):
                acc = bb
                for k in range(kk):
                    acc = acc + wb[k] * ext_ref[c, pl.ds(r0 + q * SUBLANE + off + k, SUBLANE), :]
                out_ref[pl.ds(r0 + q * SUBLANE, SUBLANE), cs] = acc
            return carry

        lax.fori_loop(0, tt // rows_per_iter, blk, 0)
        ext_ref[c, 0:halo, :] = ext_ref[c, tt:tt + halo, :]


def _prompt_mixer_body(x_ref, gmix_ref, win_ref, wca_ref, bca_ref, wra_ref, bra_ref, wrx_ref, brx_ref,
                       lam_ref, wdw_ref, bdw_ref, gln_ref, bln_ref, wpw_ref, bpw_ref, wout_ref,
                       gffn_ref, wrt_ref, br_ref,
                       h_ref, xn_ref, idx_ref, prob_ref, rank_ref, cnt_out_ref, pca_ref, ph_ref, pcf_ref,
                       xa_ext, v_ext, s0, s1, s2, s3, s4, hcar, cnt):
    tt, d = x_ref.shape
    b = pl.program_id(0)
    j = pl.program_id(1)

    @pl.when(j == 0)
    def _():
        hcar[...] = jnp.zeros_like(hcar)

    @pl.when((j == 0) & (b == 0))
    def _():
        cnt[...] = jnp.zeros_like(cnt)

    x = x_ref[...]
    hn = _rms(x, gmix_ref[...]).astype(_BF)

    xa = _dot(hn, win_ref[:, 0:d])
    _causal_conv_tile(xa, xa_ext, wca_ref, bca_ref, s4, j == 0)
    pca_ref[...] = xa[tt - CONV_A_HALO:tt, :]
    xcb = s4[...].astype(_BF)
    _block_diag(xcb, wra_ref, bra_ref, s0)
    _block_diag(xcb, wrx_ref, brx_ref, s1)
    _rglru_coeffs(s4, s0, s1, lam_ref, s2, s3)

    ri = lax.broadcasted_iota(jnp.int32, (SUBLANE, d), 0)

    def scan_blk(i, h):
        r0 = pl.multiple_of(i * SUBLANE, SUBLANE)
        a = s2[pl.ds(r0, SUBLANE), :]
        u = s3[pl.ds(r0, SUBLANE), :]
        for s in (1, 2, 4):
            keep = ri >= s
            a_sh = jnp.where(keep, pltpu.roll(a, s, axis=0), 1.0)
            u_sh = jnp.where(keep, pltpu.roll(u, s, axis=0), 0.0)
            u = a * u_sh + u
            a = a * a_sh
        hh = a * h + u
        s0[pl.ds(r0, SUBLANE), :] = hh
        return jnp.broadcast_to(hh[SUBLANE - 1:SUBLANE, :], (SUBLANE, d))

    h_last = lax.fori_loop(0, tt // SUBLANE, scan_blk, hcar[...])
    hcar[...] = h_last
    ph_ref[...] = h_last[0:1, :]

    v = _dot(hn, win_ref[:, d:2 * d]) * _sigmoid(_dot(hn, win_ref[:, 2 * d:3 * d]))
    pcf_ref[...] = v[tt - CONF_HALO:tt, :]
    _causal_conv_tile(v, v_ext, wdw_ref, bdw_ref, s1, j == 0)
    sl =_layernorm_silu(s1[...], gln_ref[...], bln_ref[...])
    yb = _dot(sl.astype(_BF), wpw_ref[...]) + bpw_ref[...]

    ga = _dot(hn, win_ref[:, 3 * d:4 * d])
    gb = _dot(hn, win_ref[:, 4 * d:5 * d])
    xn_b = _tail(x, s0[...], yb, ga, gb, wout_ref, gffn_ref, h_ref, xn_ref)
    _route(xn_b, wrt_ref, br_ref, cnt, idx_ref, prob_ref, rank_ref)
    cnt_out_ref[...] = cnt[...]


def _prompt_mixer(x, n_total, wts, tt):
    bsz, t, d = x.shape
    nt = t // tt
    ne = wts["wrt"].shape[0]
    names = ["gmix", "win", "wca", "bca", "wra", "bra", "wrx", "brx", "lam", "wdw", "bdw", "gln", "bln",
             "wpw", "bpw", "wout", "gffn", "wrt", "br"]
    w_args = [wts[n] for n in names]
    tok_spec = lambda rows: pl.BlockSpec((rows, tt), lambda b, j: (0, b * nt + j))
    out_shape = [
        jax.ShapeDtypeStruct((bsz, t, d), _F32),
        jax.ShapeDtypeStruct((n_total, d), _F32),
        jax.ShapeDtypeStruct((TOP_K, bsz * t), jnp.int32),
        jax.ShapeDtypeStruct((TOP_K, bsz * t), _F32),
        jax.ShapeDtypeStruct((TOP_K, bsz * t), jnp.int32),
        jax.ShapeDtypeStruct((ne, LANE), _F32),
        jax.ShapeDtypeStruct((bsz, CONV_A_HALO, d), _F32),
        jax.ShapeDtypeStruct((bsz, 1, d), _F32),
        jax.ShapeDtypeStruct((bsz, CONF_HALO, d), _F32),
    ]
    out_specs = [
        pl.BlockSpec((None, tt, d), lambda b, j: (b, j, 0)),
        pl.BlockSpec((tt, d), lambda b, j: (b * nt + j, 0)),
        tok_spec(TOP_K), tok_spec(TOP_K), tok_spec(TOP_K),
        pl.BlockSpec((ne, LANE), lambda b, j: (0, 0)),
        pl.BlockSpec((None, CONV_A_HALO, d), lambda b, j: (b, 0, 0)),
        pl.BlockSpec((None, 1, d), lambda b, j: (b, 0, 0)),
        pl.BlockSpec((None, CONF_HALO, d), lambda b, j: (b, 0, 0)),
    ]
    scratch = [
        pltpu.VMEM((d // LANE, tt + CONV_A_HALO, LANE), _F32),
        pltpu.VMEM((d // LANE, tt + CONF_HALO, LANE), _F32),
        pltpu.VMEM((tt, d), _F32), pltpu.VMEM((tt, d), _F32), pltpu.VMEM((tt, d), _F32),
        pltpu.VMEM((tt, d), _F32), pltpu.VMEM((tt, d), _F32),
        pltpu.VMEM((SUBLANE, d), _F32),
        pltpu.VMEM((ne, LANE), _F32),
    ]
    return pl.pallas_call(
        _prompt_mixer_body,
        grid=(bsz, nt),
        in_specs=[pl.BlockSpec((None, tt, d), lambda b, j: (b, j, 0))] + [_const_spec(w.shape) for w in w_args],
        out_specs=out_specs,
        out_shape=out_shape,
        scratch_shapes=scratch,
        compiler_params=pltpu.CompilerParams(dimension_semantics=("arbitrary", "arbitrary"),
                                             vmem_limit_bytes=52 * 1024 * 1024),
        name="prompt_mixer",
    )(x, *w_args)


def _sample_mixer_body(nb, x_ref, sca_ref, sh_ref, scf_ref, cnt_in_ref, xn_alias_ref,
                       gmix_ref, win_ref, wca_ref, bca_ref, wra_ref, bra_ref, wrx_ref, brx_ref,
                       lam_ref, wdw_ref, bdw_ref, gln_ref, bln_ref, wpw_ref, bpw_ref, wout_ref,
                       gffn_ref, wrt_ref, br_ref,
                       h_ref, xn_ref, idx_ref, prob_ref, rank_ref, cnt_out_ref, v_out_ref, sh_out_ref, xa_out_ref,
                       s0, s1, s2, s3, s4, cnt):
    del xn_alias_ref
    n, d = x_ref.shape
    steps = n // nb

    @pl.when(pl.program_id(0) == 0)
    def _():
        cnt[...] = cnt_in_ref[...]

    x = x_ref[...]
    hn = _rms(x, gmix_ref[...]).astype(_BF)

    xa = _dot(hn, win_ref[:, 0:d])
    xa_out_ref[...] = xa
    kw = wca_ref.shape[0]

    def xcat_a(slab):
        if slab < kw - 1:
            return sca_ref[slab * nb:(slab + 1) * nb, :]
        s = slab - (kw - 1)
        return xa[s * nb:(s + 1) * nb, :]

    for t in range(steps):
        acc = bca_ref[...] + jnp.zeros((nb, d), _F32)
        for k in range(kw):
            acc = acc + wca_ref[k:k + 1, :] * xcat_a(t + k)
        s4[t * nb:(t + 1) * nb, :] = acc
    xcb = s4[...].astype(_BF)
    _block_diag(xcb, wra_ref, bra_ref, s0)
    _block_diag(xcb, wrx_ref, brx_ref, s1)
    _rglru_coeffs(s4, s0, s1, lam_ref, s2, s3)
    h = sh_ref[...]
    for t in range(steps):
        h = s2[t * nb:(t + 1) * nb, :] * h + s3[t * nb:(t + 1) * nb, :]
        s0[t * nb:(t + 1) * nb, :] = h
    sh_out_ref[...] = h

    v = _dot(hn, win_ref[:, d:2 * d]) * _sigmoid(_dot(hn, win_ref[:, 2 * d:3 * d]))
    v_out_ref[...] = v
    kc = wdw_ref.shape[0]

    def xcat_v(slab):
        if slab < kc - 1:
            return scf_ref[slab * nb:(slab + 1) * nb, :]
        s = slab - (kc - 1)
        return v[s * nb:(s + 1) * nb, :]

    for t in range(steps):
        acc = bdw_ref[...] + jnp.zeros((nb, d), _F32)
        for k in range(kc):
            acc = acc + wdw_ref[k:k + 1, :] * xcat_v(t + k)
        s1[t * nb:(t + 1) * nb, :] = acc
    sl = _layernorm_silu(s1[...], gln_ref[...], bln_ref[...])
    yb = _dot(sl.astype(_BF), wpw_ref[...]) + bpw_ref[...]

    ga = _dot(hn, win_ref[:, 3 * d:4 * d])
    gb = _dot(hn, win_ref[:, 4 * d:5 * d])
    xn_b = _tail(x, s0[...], yb, ga, gb, wout_ref, gffn_ref, h_ref, xn_ref)
    _route(xn_b, wrt_ref, br_ref, cnt, idx_ref, prob_ref, rank_ref)
    cnt_out_ref[...] = cnt[...]


def _sample_mixer(x_cm, sca_cm, sh, scf_cm, cnt_in, xn_all, wts, cb, steps):
    n, d = x_cm.shape
    rows = cb * steps
    n_chunks = n // rows
    n_total = xn_all.shape[0]
    ne = wts["wrt"].shape[0]
    kw = wts["wca"].shape[0]
    kc = wts["wdw"].shape[0]
    names = ["gmix", "win", "wca", "bca", "wra", "bra", "wrx", "brx", "lam", "wdw", "bdw", "gln", "bln",
             "wpw", "bpw", "wout", "gffn", "wrt", "br"]
    w_args = [wts[n_] for n_ in names]
    chunk = lambda r: pl.BlockSpec((r, d), lambda i: (i, 0))
    tok = lambda: pl.BlockSpec((TOP_K, rows), lambda i: (0, i))
    in_specs = [chunk(rows), chunk((kw - 1) * cb), chunk(cb), chunk((kc - 1) * cb), _const_spec(cnt_in.shape),
                pl.BlockSpec(memory_space=pl.ANY)] + [_const_spec(w.shape) for w in w_args]
    out_shape = [
        jax.ShapeDtypeStruct((n, d), _F32),
        jax.ShapeDtypeStruct((n_total, d), _F32),
        jax.ShapeDtypeStruct((TOP_K, n), jnp.int32),
        jax.ShapeDtypeStruct((TOP_K, n), _F32),
        jax.ShapeDtypeStruct((TOP_K, n), jnp.int32),
        jax.ShapeDtypeStruct((ne, LANE), _F32),
        jax.ShapeDtypeStruct((n, d), _F32),
        jax.ShapeDtypeStruct(sh.shape, _F32),
        jax.ShapeDtypeStruct((n, d), _F32),
    ]
    assert (n_total - n) % rows == 0
    first = (n_total - n) // rows
    out_specs = [
        chunk(rows),
        pl.BlockSpec((rows, d), lambda i: (first + i, 0)),
        tok(), tok(), tok(),
        pl.BlockSpec((ne, LANE), lambda i: (0, 0)),
        chunk(rows), chunk(cb), chunk(rows),
    ]
    scratch = [pltpu.VMEM((rows, d), _F32)] * 5 + [pltpu.VMEM((ne, LANE), _F32)]
    return pl.pallas_call(
        functools.partial(_sample_mixer_body, cb),
        grid=(n_chunks,),
        in_specs=in_specs,
        out_specs=out_specs,
        out_shape=out_shape,
        scratch_shapes=scratch,
        input_output_aliases={5: 1},
        compiler_params=pltpu.CompilerParams(dimension_semantics=("arbitrary",),
                                             vmem_limit_bytes=52 * 1024 * 1024),
        name="sample_mixer",
    )(x_cm, sca_cm, sh, scf_cm, cnt_in, xn_all, *w_args)


def _dispatch_body(offs_ref, idx_ref, rank_ref, xn_ref, xs_ref, sem):
    tt = xn_ref.shape[0]

    def row_copy(t, k):
        dest = offs_ref[idx_ref[k, t]] + rank_ref[k, t]
        return pltpu.make_async_copy(xn_ref.at[pl.ds(t, 1), :], xs_ref.at[pl.ds(dest, 1), :], sem)

    def issue(t, c):
        for k in range(TOP_K):
            row_copy(t, k).start()
        return c

    lax.fori_loop(0, tt, issue, 0)
    for k in range(TOP_K):
        pltpu.make_async_copy(xn_ref, xs_ref.at[pl.ds(0, tt), :], sem).wait()


def _dispatch(offs, idx, rank, xn, n_rows, tt):
    n, d = xn.shape
    smem_tok = pl.BlockSpec((TOP_K, tt), lambda i, offs: (0, i), memory_space=pltpu.SMEM)
    return pl.pallas_call(
        _dispatch_body,
        grid_spec=pltpu.PrefetchScalarGridSpec(
            num_scalar_prefetch=1,
            grid=(n // tt,),
            in_specs=[smem_tok, smem_tok, pl.BlockSpec((tt, d), lambda i, offs: (i, 0))],
            out_specs=pl.BlockSpec(memory_space=pl.ANY),
            scratch_shapes=[pltpu.SemaphoreType.DMA(())],
        ),
        out_shape=jax.ShapeDtypeStruct((n_rows, d), _F32),
        compiler_params=pltpu.CompilerParams(dimension_semantics=("arbitrary",)),
        name="dispatch",
    )(offs, idx, rank, xn)


def _expert_ffn_body(te_ref, nu_ref, xs_ref, wg_ref, bg_ref, wu_ref, bu_ref, wd_ref, bd_ref, ys_ref,
                     wg_b, wu_b, wd_b):
    i = pl.program_id(0)
    prev = te_ref[jnp.maximum(i - 1, 0)]
    fresh = (i == 0) | (te_ref[i] != prev)

    @pl.when(fresh & (i < nu_ref[0]))
    def _():
        wg_b[...] = wg_ref[...].astype(_BF)
        wu_b[...] = wu_ref[...].astype(_BF)
        wd_b[...] = wd_ref[...].astype(_BF)

    @pl.when(i < nu_ref[0])
    def _():
        x = xs_ref[...].astype(_BF)
        g = jnp.minimum(_dot(x, wg_b[...]) + bg_ref[...], SWIGLU_LIMIT)
        u = jnp.clip(_dot(x, wu_b[...]) + bu_ref[...], -SWIGLU_LIMIT, SWIGLU_LIMIT)
        hdn = (u + 1.0) * (g * _sigmoid(SWIGLU_ALPHA * g))
        ys_ref[...] = _dot(hdn.astype(_BF), wd_b[...]) + bd_ref[...]


def _expert_ffn(tile_expert, n_used, xs, wg, bg, wu, bu, wd, bd, tm):
    p, d = xs.shape
    ne, _, de = wg.shape
    n_tiles = p // tm
    row = lambda i, te, nu: (jnp.minimum(i, nu[0] - 1), 0)
    wspec = lambda a, b_: pl.BlockSpec((None, a, b_), lambda i, te, nu: (te[i], 0, 0))
    return pl.pallas_call(
        _expert_ffn_body,
        grid_spec=pltpu.PrefetchScalarGridSpec(
            num_scalar_prefetch=2,
            grid=(n_tiles,),
            in_specs=[pl.BlockSpec((tm, d), row),
                      wspec(d, de), wspec(1, de), wspec(d, de), wspec(1, de), wspec(de, d), wspec(1, d)],
            out_specs=pl.BlockSpec((tm, d), row),
            scratch_shapes=[pltpu.VMEM((d, de), _BF), pltpu.VMEM((d, de), _BF), pltpu.VMEM((de, d), _BF)],
        ),
        out_shape=jax.ShapeDtypeStruct((p, d), _F32),
        compiler_params=pltpu.CompilerParams(dimension_semantics=("arbitrary",),
                                             vmem_limit_bytes=56 * 1024 * 1024),
        name="expert_ffn",
    )(tile_expert, n_used, xs, wg, bg, wu, bu, wd, bd)


def _combine_body(offs_ref, idx_ref, rank_ref, h_ref, pt_ref, gfin_ref, ys_ref, y_ref, gbuf, sem):
    tt, d = h_ref.shape

    def row_copy(t, k):
        src = offs_ref[idx_ref[k, t]] + rank_ref[k, t]
        return pltpu.make_async_copy(ys_ref.at[pl.ds(src, 1), :], gbuf.at[k, pl.ds(t, 1), :], sem)

    def issue(t, c):
        for k in range(TOP_K):
            row_copy(t, k).start()
        return c

    lax.fori_loop(0, tt, issue, 0)
    for k in range(TOP_K):
        pltpu.make_async_copy(ys_ref.at[pl.ds(0, tt), :], gbuf.at[k], sem).wait()
    out = h_ref[...]
    for k in range(TOP_K):
        out = out + pt_ref[:, k:k + 1] * gbuf[k]
    y_ref[...] = _rms(out, gfin_ref[...])


def _combine(offs, idx, rank, h, prob_t, g_final, ys, tt, col0):
    n, d = h.shape
    smem_tok = pl.BlockSpec((TOP_K, tt), lambda i, offs: (0, col0 + i), memory_space=pltpu.SMEM)
    return pl.pallas_call(
        _combine_body,
        grid_spec=pltpu.PrefetchScalarGridSpec(
            num_scalar_prefetch=1,
            grid=(n // tt,),
            in_specs=[smem_tok, smem_tok,
                      pl.BlockSpec((tt, d), lambda i, offs: (i, 0)),
                      pl.BlockSpec((tt, TOP_K), lambda i, offs: (col0 + i, 0)),
                      pl.BlockSpec((1, d), lambda i, offs: (0, 0)),
                      pl.BlockSpec(memory_space=pl.ANY)],
            out_specs=pl.BlockSpec((tt, d), lambda i, offs: (i, 0)),
            scratch_shapes=[pltpu.VMEM((TOP_K, tt, d), _F32), pltpu.SemaphoreType.DMA(())],
        ),
        out_shape=jax.ShapeDtypeStruct((n, d), _F32),
        compiler_params=pltpu.CompilerParams(dimension_semantics=("arbitrary",)),
        name="combine",
    )(offs, idx, rank, h, prob_t, g_final, ys)


def _tile_sizes(seq, nb):
    return min(256, seq), min(64, nb), 256, 512


def kernel(x_prompt, x_sample, state_conv_a, state_h, state_conf, g_mix, w_in, w_conv_a, b_conv_a,
           w_rg_a, b_rg_a, w_rg_x, b_rg_x, rg_lambda, w_conf_dw, b_conf_dw, g_conf_ln, b_conf_ln,
           w_conf_pw, b_conf_pw, w_out, g_ffn, w_router, b_router, w_gate, b_gate, w_up, b_up,
           w_down, b_down, g_final):
    depth = g_mix.shape[0]
    assert depth == 1
    bsz, seq, d = x_prompt.shape
    nb, steps, _ = x_sample.shape
    ne = w_router.shape[-1]
    kw = w_conv_a.shape[1]
    kc = w_conf_dw.shape[1]
    n_p = bsz * seq
    n_s = nb * steps
    n_all = n_p + n_s
    t_mix, cb, t_tok, tm = _tile_sizes(seq, nb)
    assert seq % t_mix == 0 and nb % cb == 0 and n_p % t_tok == 0 and n_s % t_tok == 0

    row = lambda a: a[0].reshape(1, -1)
    wts = dict(
        gmix=row(g_mix), win=w_in[0].astype(_BF), wca=w_conv_a[0], bca=row(b_conv_a),
        wra=w_rg_a[0].astype(_BF), bra=row(b_rg_a), wrx=w_rg_x[0].astype(_BF), brx=row(b_rg_x),
        lam=row(rg_lambda), wdw=w_conf_dw[0], bdw=row(b_conf_dw), gln=row(g_conf_ln), bln=row(b_conf_ln),
        wpw=w_conf_pw[0].astype(_BF), bpw=row(b_conf_pw), wout=w_out[0].astype(_BF), gffn=row(g_ffn),
        wrt=w_router[0].T.astype(_BF), br=b_router[0].reshape(ne, 1),
    )

    (h_p, xn_all, idx_p, prob_p, rank_p, cnt_p, pca, ph, pcf) = _prompt_mixer(x_prompt, n_all, wts, t_mix)
    to_cm = lambda a: jnp.swapaxes(a.reshape(nb // cb, cb, a.shape[1], d), 1, 2).reshape(-1, d)
    (h_s, xn_all, idx_s, prob_s, rank_s, cnt, v_s, sh_new, xa_s) = _sample_mixer(
        to_cm(x_sample), to_cm(state_conv_a[0]), state_h[0], to_cm(state_conf[0]), cnt_p, xn_all,
        wts, cb, steps)

    idx = jnp.concatenate([idx_p, idx_s], axis=1)
    rank = jnp.concatenate([rank_p, rank_s], axis=1)
    prob_t = jnp.concatenate([prob_p, prob_s], axis=1).T
    counts = cnt[:, 0].astype(jnp.int32)
    tiles_e = (counts + tm - 1) // tm
    tile_end = jnp.cumsum(tiles_e)
    offs = (tile_end - tiles_e) * tm
    n_tiles = (n_all * TOP_K) // tm + ne
    tile_expert = jnp.minimum(jnp.searchsorted(tile_end, jnp.arange(n_tiles, dtype=jnp.int32), side="right"),
                              ne - 1).astype(jnp.int32)
    n_used = tile_end[-1:].astype(jnp.int32)

    xs = _dispatch(offs, idx, rank, xn_all, n_tiles * tm, t_tok)
    ys = _expert_ffn(tile_expert, n_used, xs, w_gate[0], b_gate[0].reshape(ne, 1, -1), w_up[0],
                     b_up[0].reshape(ne, 1, -1), w_down[0], b_down[0].reshape(ne, 1, -1), tm)
    gfin = g_final.reshape(1, d)
    y_p = _combine(offs, idx, rank, h_p.reshape(n_p, d), prob_t, gfin, ys, t_tok, 0)
    y_s = _combine(offs, idx, rank, h_s, prob_t, gfin, ys, t_tok, n_p // t_tok)

    nat = lambda a: jnp.swapaxes(a.reshape(nb // cb, steps, cb, d), 1, 2).reshape(nb, steps, d)
    y_prompt = y_p.reshape(bsz, seq, d)
    y_sample = nat(y_s)
    p_conv_a = pca[:, CONV_A_HALO - (kw - 1):, :][None]
    p_h = ph.reshape(1, bsz, d)
    p_conf = pcf[:, CONF_HALO - (kc - 1):, :][None]
    s_conv_a = jnp.concatenate([state_conv_a[0], nat(xa_s)], axis=1)[:, -(kw - 1):, :][None]
    s_h = sh_new[None]
    s_conf = jnp.concatenate([state_conf[0], nat(v_s)], axis=1)[:, -(kc - 1):, :][None]
    return (y_prompt, y_sample, p_conv_a, p_h, p_conf, s_conv_a, s_h, s_conf)
```

```python
import functools

import jax
import jax.numpy as jnp
from jax import lax
from jax.experimental import pallas as pl
from jax.experimental.pallas import tpu as pltpu

EPS = 1e-6
RG_C = 8.0
SWIGLU_LIMIT = 7.0
SWIGLU_ALPHA = 1.702
TOP_K = 4
LANE = 128
SUBLANE = 8
CONV_A_HALO = 8
CONF_HALO = 32

_BF = jnp.bfloat16
_F32 = jnp.float32


def _const_spec(shape):
    nd = len(shape)
    return pl.BlockSpec(shape, lambda *_: (0,) * nd, pipeline_mode=pl.Buffered(1))


def _rms(x, g):
    return x * lax.rsqrt(jnp.mean(x * x, axis=-1, keepdims=True) + EPS) * g


def _sigmoid(x):
    return jax.nn.sigmoid(x)


def _dot(a, b):
    return jnp.dot(a, b, preferred_element_type=_F32)


def _block_diag(xb, w_ref, b_ref, out_ref):
    nb, gb, _ = w_ref.shape
    for n in range(nb):
        cs = slice(n * gb, (n + 1) * gb)
        out_ref[:, cs] = _dot(xb[:, cs], w_ref[n]) + b_ref[:, cs]


def _rglru_coeffs(xc_ref, r_ref, i_ref, lam_ref, a_ref, u_ref):
    sp = jax.nn.softplus(-lam_ref[...])
    r = _sigmoid(r_ref[...])
    i = _sigmoid(i_ref[...])
    log_a = -RG_C * r * sp
    a = jnp.exp(log_a)
    a_ref[...] = a
    u_ref[...] = jnp.sqrt(-jnp.tanh(log_a) * (a * a + 1.0)) * (i * xc_ref[...])


def _layernorm_silu(vc, g, b):
    mu = jnp.mean(vc, axis=-1, keepdims=True)
    xc = vc - mu
    y = xc * lax.rsqrt(jnp.mean(xc * xc, axis=-1, keepdims=True) + EPS) * g + b
    return y * _sigmoid(y)


def _route(xn_b, wrt_ref, br_ref, cnt_ref, idx_ref, prob_ref, rank_ref):
    ne = wrt_ref.shape[0]
    tt = xn_b.shape[0]
    logits = lax.dot_general(wrt_ref[...], xn_b, (((1,), (1,)), ((), ())),
                             preferred_element_type=_F32) + br_ref[...]
    eidx = lax.broadcasted_iota(jnp.int32, (ne, tt), 0).astype(_F32)
    work = logits
    vals, picks, onehots = [], [], []
    for _ in range(TOP_K):
        m = jnp.max(work, axis=0, keepdims=True)
        pick = jnp.min(jnp.where(work == m, eidx, float(ne)), axis=0, keepdims=True)
        one = eidx == pick
        work = jnp.where(one, -jnp.inf, work)
        vals.append(m)
        picks.append(pick)
        onehots.append(one)
    exps = [jnp.exp(v - vals[0]) for v in vals]
    denom = exps[0] + exps[1] + exps[2] + exps[3]
    inv = 1.0 / denom
    sel = jnp.zeros((ne, tt), _F32)
    for one in onehots:
        sel = sel + one.astype(_F32)
    row = lax.broadcasted_iota(jnp.int32, (tt, tt), 0)
    col = lax.broadcasted_iota(jnp.int32, (tt, tt), 1)
    tri = (row <= col).astype(_BF)
    incl = _dot(sel.astype(_BF), tri)
    base = cnt_ref[:, 0:1]
    before = base + (incl - sel)
    for k in range(TOP_K):
        idx_ref[k:k + 1, :] = picks[k].astype(jnp.int32)
        prob_ref[k:k + 1, :] = exps[k] * inv
        rk = jnp.sum(jnp.where(onehots[k], before, 0.0), axis=0, keepdims=True)
        rank_ref[k:k + 1, :] = rk.astype(jnp.int32)
    cnt_ref[...] = cnt_ref[...] + incl[:, tt - 1:tt]


def _tail(x, ya, yb, ga, gb, wout_ref, gffn_ref, h_ref, xn_ref):
    m = _sigmoid(ga) * ya + _sigmoid(gb) * yb
    h = x + _dot(m.astype(_BF), wout_ref[...])
    h_ref[...] = h
    xn = _rms(h, gffn_ref[...])
    xn_ref[...] = xn
    return xn.astype(_BF)


def _causal_conv_tile(x, ext_ref, w_ref, b_ref, out_ref, first):
    tt, d = x.shape
    halo = ext_ref.shape[1] - tt
    kk = w_ref.shape[0]
    off = halo - (kk - 1)
    groups = 4
    rows_per_iter = groups * SUBLANE
    for c in range(d // LANE):
        cs = slice(c * LANE, (c + 1) * LANE)

        @pl.when(first)
        def _():
            ext_ref[c, 0:halo, :] = jnp.zeros((halo, LANE), _F32)

        ext_ref[c, halo:halo + tt, :] = x[:, cs]
        wb = [jnp.broadcast_to(w_ref[k:k + 1, cs], (SUBLANE, LANE)) for k in range(kk)]
        bb = jnp.broadcast_to(b_ref[0:1, cs], (SUBLANE, LANE))

        def blk(i, carry, c=c, cs=cs, wb=wb, bb=bb):
            r0 = pl.multiple_of(i * rows_per_iter, rows_per_iter)
            for q in range(groups):
                acc = bb
                for k in range(kk):
                    acc = acc + wb[k] * ext_ref[c, pl.ds(r0 + q * SUBLANE + off + k, SUBLANE), :]
                out_ref[pl.ds(r0 + q * SUBLANE, SUBLANE), cs] = acc
            return carry

        lax.fori_loop(0, tt // rows_per_iter, blk, 0)
        ext_ref[c, 0:halo, :] = ext_ref[c, tt:tt + halo, :]


def _prompt_mixer_body(x_ref, gmix_ref, win_ref, wca_ref, bca_ref, wra_ref, bra_ref, wrx_ref, brx_ref,
                       lam_ref, wdw_ref, bdw_ref, gln_ref, bln_ref, wpw_ref, bpw_ref, wout_ref,
                       gffn_ref, wrt_ref, br_ref,
                       h_ref, xn_ref, idx_ref, prob_ref, rank_ref, cnt_out_ref, pca_ref, ph_ref, pcf_ref,
                       xa_ext, v_ext, s0, s1, s2, s3, s4, hcar, cnt):
    tt, d = x_ref.shape
    b = pl.program_id(0)
    j = pl.program_id(1)

    @pl.when(j == 0)
    def _():
        hcar[...] = jnp.zeros_like(hcar)

    @pl.when((j == 0) & (b == 0))
    def _():
        cnt[...] = jnp.zeros_like(cnt)

    x = x_ref[...]
    hn = _rms(x, gmix_ref[...]).astype(_BF)

    xa = _dot(hn, win_ref[:, 0:d])
    _causal_conv_tile(xa, xa_ext, wca_ref, bca_ref, s4, j == 0)
    pca_ref[...] = xa[tt - CONV_A_HALO:tt, :]
    xcb = s4[...].astype(_BF)
    _block_diag(xcb, wra_ref, bra_ref, s0)
    _block_diag(xcb, wrx_ref, brx_ref, s1)
    _rglru_coeffs(s4, s0, s1, lam_ref, s2, s3)

    ri = lax.broadcasted_iota(jnp.int32, (SUBLANE, d), 0)

    def scan_blk(i, h):
        r0 = pl.multiple_of(i * SUBLANE, SUBLANE)
        a = s2[pl.ds(r0, SUBLANE), :]
        u = s3[pl.ds(r0, SUBLANE), :]
        for s in (1, 2, 4):
            keep = ri >= s
            a_sh = jnp.where(keep, pltpu.roll(a, s, axis=0), 1.0)
            u_sh = jnp.where(keep, pltpu.roll(u, s, axis=0), 0.0)
            u = a * u_sh + u
            a = a * a_sh
        hh = a * h + u
        s0[pl.ds(r0, SUBLANE), :] = hh
        return jnp.broadcast_to(hh[SUBLANE - 1:SUBLANE, :], (SUBLANE, d))

    h_last = lax.fori_loop(0, tt // SUBLANE, scan_blk, hcar[...])
    hcar[...] = h_last
    ph_ref[...] = h_last[0:1, :]

    v = _dot(hn, win_ref[:, d:2 * d]) * _sigmoid(_dot(hn, win_ref[:, 2 * d:3 * d]))
    pcf_ref[...] = v[tt - CONF_HALO:tt, :]
    _causal_conv_tile(v, v_ext, wdw_ref, bdw_ref, s1, j == 0)
    sl =_layernorm_silu(s1[...], gln_ref[...], bln_ref[...])
    yb = _dot(sl.astype(_BF), wpw_ref[...]) + bpw_ref[...]

    ga = _dot(hn, win_ref[:, 3 * d:4 * d])
    gb = _dot(hn, win_ref[:, 4 * d:5 * d])
    xn_b = _tail(x, s0[...], yb, ga, gb, wout_ref, gffn_ref, h_ref, xn_ref)
    _route(xn_b, wrt_ref, br_ref, cnt, idx_ref, prob_ref, rank_ref)
    cnt_out_ref[...] = cnt[...]


def _prompt_mixer(x, n_total, wts, tt):
    bsz, t, d = x.shape
    nt = t // tt
    ne = wts["wrt"].shape[0]
    names = ["gmix", "win", "wca", "bca", "wra", "bra", "wrx", "brx", "lam", "wdw", "bdw", "gln", "bln",
             "wpw", "bpw", "wout", "gffn", "wrt", "br"]
    w_args = [wts[n] for n in names]
    tok_spec = lambda rows: pl.BlockSpec((rows, tt), lambda b, j: (0, b * nt + j))
    out_shape = [
        jax.ShapeDtypeStruct((bsz, t, d), _F32),
        jax.ShapeDtypeStruct((n_total, d), _F32),
        jax.ShapeDtypeStruct((TOP_K, bsz * t), jnp.int32),
        jax.ShapeDtypeStruct((TOP_K, bsz * t), _F32),
        jax.ShapeDtypeStruct((TOP_K, bsz * t), jnp.int32),
        jax.ShapeDtypeStruct((ne, LANE), _F32),
        jax.ShapeDtypeStruct((bsz, CONV_A_HALO, d), _F32),
        jax.ShapeDtypeStruct((bsz, 1, d), _F32),
        jax.ShapeDtypeStruct((bsz, CONF_HALO, d), _F32),
    ]
    out_specs = [
        pl.BlockSpec((None, tt, d), lambda b, j: (b, j, 0)),
        pl.BlockSpec((tt, d), lambda b, j: (b * nt + j, 0)),
        tok_spec(TOP_K), tok_spec(TOP_K), tok_spec(TOP_K),
        pl.BlockSpec((ne, LANE), lambda b, j: (0, 0)),
        pl.BlockSpec((None, CONV_A_HALO, d), lambda b, j: (b, 0, 0)),
        pl.BlockSpec((None, 1, d), lambda b, j: (b, 0, 0)),
        pl.BlockSpec((None, CONF_HALO, d), lambda b, j: (b, 0, 0)),
    ]
    scratch = [
        pltpu.VMEM((d // LANE, tt + CONV_A_HALO, LANE), _F32),
        pltpu.VMEM((d // LANE, tt + CONF_HALO, LANE), _F32),
        pltpu.VMEM((tt, d), _F32), pltpu.VMEM((tt, d), _F32), pltpu.VMEM((tt, d), _F32),
        pltpu.VMEM((tt, d), _F32), pltpu.VMEM((tt, d), _F32),
        pltpu.VMEM((SUBLANE, d), _F32),
        pltpu.VMEM((ne, LANE), _F32),
    ]
    return pl.pallas_call(
        _prompt_mixer_body,
        grid=(bsz, nt),
        in_specs=[pl.BlockSpec((None, tt, d), lambda b, j: (b, j, 0))] + [_const_spec(w.shape) for w in w_args],
        out_specs=out_specs,
        out_shape=out_shape,
        scratch_shapes=scratch,
        compiler_params=pltpu.CompilerParams(dimension_semantics=("arbitrary", "arbitrary"),
                                             vmem_limit_bytes=52 * 1024 * 1024),
        name="prompt_mixer",
    )(x, *w_args)


def _sample_mixer_body(nb, x_ref, sca_ref, sh_ref, scf_ref, cnt_in_ref, xn_alias_ref,
                       gmix_ref, win_ref, wca_ref, bca_ref, wra_ref, bra_ref, wrx_ref, brx_ref,
                       lam_ref, wdw_ref, bdw_ref, gln_ref, bln_ref, wpw_ref, bpw_ref, wout_ref,
                       gffn_ref, wrt_ref, br_ref,
                       h_ref, xn_ref, idx_ref, prob_ref, rank_ref, cnt_out_ref, v_out_ref, sh_out_ref, xa_out_ref,
                       s0, s1, s2, s3, s4, cnt):
    del xn_alias_ref
    n, d = x_ref.shape
    steps = n // nb

    @pl.when(pl.program_id(0) == 0)
    def _():
        cnt[...] = cnt_in_ref[...]

    x = x_ref[...]
    hn = _rms(x, gmix_ref[...]).astype(_BF)

    xa = _dot(hn, win_ref[:, 0:d])
    xa_out_ref[...] = xa
    kw = wca_ref.shape[0]

    def xcat_a(slab):
        if slab < kw - 1:
            return sca_ref[slab * nb:(slab + 1) * nb, :]
        s = slab - (kw - 1)
        return xa[s * nb:(s + 1) * nb, :]

    for t in range(steps):
        acc = bca_ref[...] + jnp.zeros((nb, d), _F32)
        for k in range(kw):
            acc = acc + wca_ref[k:k + 1, :] * xcat_a(t + k)
        s4[t * nb:(t + 1) * nb, :] = acc
    xcb = s4[...].astype(_BF)
    _block_diag(xcb, wra_ref, bra_ref, s0)
    _block_diag(xcb, wrx_ref, brx_ref, s1)
    _rglru_coeffs(s4, s0, s1, lam_ref, s2, s3)
    h = sh_ref[...]
    for t in range(steps):
        h = s2[t * nb:(t + 1) * nb, :] * h + s3[t * nb:(t + 1) * nb, :]
        s0[t * nb:(t + 1) * nb, :] = h
    sh_out_ref[...] = h

    v = _dot(hn, win_ref[:, d:2 * d]) * _sigmoid(_dot(hn, win_ref[:, 2 * d:3 * d]))
    v_out_ref[...] = v
    kc = wdw_ref.shape[0]

    def xcat_v(slab):
        if slab < kc - 1:
            return scf_ref[slab * nb:(slab + 1) * nb, :]
        s = slab - (kc - 1)
        return v[s * nb:(s + 1) * nb, :]

    for t in range(steps):
        acc = bdw_ref[...] + jnp.zeros((nb, d), _F32)
        for k in range(kc):
            acc = acc + wdw_ref[k:k + 1, :] * xcat_v(t + k)
        s1[t * nb:(t + 1) * nb, :] = acc
    sl = _layernorm_silu(s1[...], gln_ref[...], bln_ref[...])
    yb = _dot(sl.astype(_BF), wpw_ref[...]) + bpw_ref[...]

    ga = _dot(hn, win_ref[:, 3 * d:4 * d])
    gb = _dot(hn, win_ref[:, 4 * d:5 * d])
    xn_b = _tail(x, s0[...], yb, ga, gb, wout_ref, gffn_ref, h_ref, xn_ref)
    _route(xn_b, wrt_ref, br_ref, cnt, idx_ref, prob_ref, rank_ref)
    cnt_out_ref[...] = cnt[...]


def _sample_mixer(x_cm, sca_cm, sh, scf_cm, cnt_in, xn_all, wts, cb, steps):
    n, d = x_cm.shape
    rows = cb * steps
    n_chunks = n // rows
    n_total = xn_all.shape[0]
    ne = wts["wrt"].shape[0]
    kw = wts["wca"].shape[0]
    kc = wts["wdw"].shape[0]
    names = ["gmix", "win", "wca", "bca", "wra", "bra", "wrx", "brx", "lam", "wdw", "bdw", "gln", "bln",
             "wpw", "bpw", "wout", "gffn", "wrt", "br"]
    w_args = [wts[n_] for n_ in names]
    chunk = lambda r: pl.BlockSpec((r, d), lambda i: (i, 0))
    tok = lambda: pl.BlockSpec((TOP_K, rows), lambda i: (0, i))
    in_specs = [chunk(rows), chunk((kw - 1) * cb), chunk(cb), chunk((kc - 1) * cb), _const_spec(cnt_in.shape),
                pl.BlockSpec(memory_space=pl.ANY)] + [_const_spec(w.shape) for w in w_args]
    out_shape = [
        jax.ShapeDtypeStruct((n, d), _F32),
        jax.ShapeDtypeStruct((n_total, d), _F32),
        jax.ShapeDtypeStruct((TOP_K, n), jnp.int32),
        jax.ShapeDtypeStruct((TOP_K, n), _F32),
        jax.ShapeDtypeStruct((TOP_K, n), jnp.int32),
        jax.ShapeDtypeStruct((ne, LANE), _F32),
        jax.ShapeDtypeStruct((n, d), _F32),
        jax.ShapeDtypeStruct(sh.shape, _F32),
        jax.ShapeDtypeStruct((n, d), _F32),
    ]
    assert (n_total - n) % rows == 0
    first = (n_total - n) // rows
    out_specs = [
        chunk(rows),
        pl.BlockSpec((rows, d), lambda i: (first + i, 0)),
        tok(), tok(), tok(),
        pl.BlockSpec((ne, LANE), lambda i: (0, 0)),
        chunk(rows), chunk(cb), chunk(rows),
    ]
    scratch = [pltpu.VMEM((rows, d), _F32)] * 5 + [pltpu.VMEM((ne, LANE), _F32)]
    return pl.pallas_call(
        functools.partial(_sample_mixer_body, cb),
        grid=(n_chunks,),
        in_specs=in_specs,
        out_specs=out_specs,
        out_shape=out_shape,
        scratch_shapes=scratch,
        input_output_aliases={5: 1},
        compiler_params=pltpu.CompilerParams(dimension_semantics=("arbitrary",),
                                             vmem_limit_bytes=52 * 1024 * 1024),
        name="sample_mixer",
    )(x_cm, sca_cm, sh, scf_cm, cnt_in, xn_all, *w_args)


def _dispatch_body(offs_ref, idx_ref, rank_ref, xn_ref, xs_ref, sem):
    tt = xn_ref.shape[0]

    def row_copy(t, k):
        dest = offs_ref[idx_ref[k, t]] + rank_ref[k, t]
        return pltpu.make_async_copy(xn_ref.at[pl.ds(t, 1), :], xs_ref.at[pl.ds(dest, 1), :], sem)

    def issue(t, c):
        for k in range(TOP_K):
            row_copy(t, k).start()
        return c

    lax.fori_loop(0, tt, issue, 0)
    for k in range(TOP_K):
        pltpu.make_async_copy(xn_ref, xs_ref.at[pl.ds(0, tt), :], sem).wait()


def _dispatch(offs, idx, rank, xn, n_rows, tt):
    n, d = xn.shape
    smem_tok = pl.BlockSpec((TOP_K, tt), lambda i, offs: (0, i), memory_space=pltpu.SMEM)
    return pl.pallas_call(
        _dispatch_body,
        grid_spec=pltpu.PrefetchScalarGridSpec(
            num_scalar_prefetch=1,
            grid=(n // tt,),
            in_specs=[smem_tok, smem_tok, pl.BlockSpec((tt, d), lambda i, offs: (i, 0))],
            out_specs=pl.BlockSpec(memory_space=pl.ANY),
            scratch_shapes=[pltpu.SemaphoreType.DMA(())],
        ),
        out_shape=jax.ShapeDtypeStruct((n_rows, d), _F32),
        compiler_params=pltpu.CompilerParams(dimension_semantics=("arbitrary",)),
        name="dispatch",
    )(offs, idx, rank, xn)


def _expert_ffn_body(te_ref, nu_ref, xs_ref, wg_ref, bg_ref, wu_ref, bu_ref, wd_ref, bd_ref, ys_ref,
                     wg_b, wu_b, wd_b):
    i = pl.program_id(0)
    prev = te_ref[jnp.maximum(i - 1, 0)]
    fresh = (i == 0) | (te_ref[i] != prev)

    @pl.when(fresh & (i < nu_ref[0]))
    def _():
        wg_b[...] = wg_ref[...].astype(_BF)
        wu_b[...] = wu_ref[...].astype(_BF)
        wd_b[...] = wd_ref[...].astype(_BF)

    @pl.when(i < nu_ref[0])
    def _():
        x = xs_ref[...].astype(_BF)
        g = jnp.minimum(_dot(x, wg_b[...]) + bg_ref[...], SWIGLU_LIMIT)
        u = jnp.clip(_dot(x, wu_b[...]) + bu_ref[...], -SWIGLU_LIMIT, SWIGLU_LIMIT)
        hdn = (u + 1.0) * (g * _sigmoid(SWIGLU_ALPHA * g))
        ys_ref[...] = _dot(hdn.astype(_BF), wd_b[...]) + bd_ref[...]


def _expert_ffn(tile_expert, n_used, xs, wg, bg, wu, bu, wd, bd, tm):
    p, d = xs.shape
    ne, _, de = wg.shape
    n_tiles = p // tm
    row = lambda i, te, nu: (jnp.minimum(i, nu[0] - 1), 0)
    wspec = lambda a, b_: pl.BlockSpec((None, a, b_), lambda i, te, nu: (te[i], 0, 0))
    return pl.pallas_call(
        _expert_ffn_body,
        grid_spec=pltpu.PrefetchScalarGridSpec(
            num_scalar_prefetch=2,
            grid=(n_tiles,),
            in_specs=[pl.BlockSpec((tm, d), row),
                      wspec(d, de), wspec(1, de), wspec(d, de), wspec(1, de), wspec(de, d), wspec(1, d)],
            out_specs=pl.BlockSpec((tm, d), row),
            scratch_shapes=[pltpu.VMEM((d, de), _BF), pltpu.VMEM((d, de), _BF), pltpu.VMEM((de, d), _BF)],
        ),
        out_shape=jax.ShapeDtypeStruct((p, d), _F32),
        compiler_params=pltpu.CompilerParams(dimension_semantics=("arbitrary",),
                                             vmem_limit_bytes=56 * 1024 * 1024),
        name="expert_ffn",
    )(tile_expert, n_used, xs, wg, bg, wu, bu, wd, bd)


def _combine_body(offs_ref, idx_ref, rank_ref, h_ref, pt_ref, gfin_ref, ys_ref, y_ref, gbuf, sem):
    tt, d = h_ref.shape

    def row_copy(t, k):
        src = offs_ref[idx_ref[k, t]] + rank_ref[k, t]
        return pltpu.make_async_copy(ys_ref.at[pl.ds(src, 1), :], gbuf.at[k, pl.ds(t, 1), :], sem)

    def issue(t, c):
        for k in range(TOP_K):
            row_copy(t, k).start()
        return c

    lax.fori_loop(0, tt, issue, 0)
    for k in range(TOP_K):
        pltpu.make_async_copy(ys_ref.at[pl.ds(0, tt), :], gbuf.at[k], sem).wait()
    out = h_ref[...]
    for k in range(TOP_K):
        out = out + pt_ref[:, k:k + 1] * gbuf[k]
    y_ref[...] = _rms(out, gfin_ref[...])


def _combine(offs, idx, rank, h, prob_t, g_final, ys, tt, col0):
    n, d = h.shape
    smem_tok = pl.BlockSpec((TOP_K, tt), lambda i, offs: (0, col0 + i), memory_space=pltpu.SMEM)
    return pl.pallas_call(
        _combine_body,
        grid_spec=pltpu.PrefetchScalarGridSpec(
            num_scalar_prefetch=1,
            grid=(n // tt,),
            in_specs=[smem_tok, smem_tok,
                      pl.BlockSpec((tt, d), lambda i, offs: (i, 0)),
                      pl.BlockSpec((tt, TOP_K), lambda i, offs: (col0 + i, 0)),
                      pl.BlockSpec((1, d), lambda i, offs: (0, 0)),
                      pl.BlockSpec(memory_space=pl.ANY)],
            out_specs=pl.BlockSpec((tt, d), lambda i, offs: (i, 0)),
            scratch_shapes=[pltpu.VMEM((TOP_K, tt, d), _F32), pltpu.SemaphoreType.DMA(())],
        ),
        out_shape=jax.ShapeDtypeStruct((n, d), _F32),
        compiler_params=pltpu.CompilerParams(dimension_semantics=("arbitrary",)),
        name="combine",
    )(offs, idx, rank, h, prob_t, g_final, ys)


def _tile_sizes(seq, nb):
    return min(256, seq), min(64, nb), 256, 512


def kernel(x_prompt, x_sample, state_conv_a, state_h, state_conf, g_mix, w_in, w_conv_a, b_conv_a,
           w_rg_a, b_rg_a, w_rg_x, b_rg_x, rg_lambda, w_conf_dw, b_conf_dw, g_conf_ln, b_conf_ln,
           w_conf_pw, b_conf_pw, w_out, g_ffn, w_router, b_router, w_gate, b_gate, w_up, b_up,
           w_down, b_down, g_final):
    depth = g_mix.shape[0]
    assert depth == 1
    bsz, seq, d = x_prompt.shape
    nb, steps, _ = x_sample.shape
    ne = w_router.shape[-1]
    kw = w_conv_a.shape[1]
    kc = w_conf_dw.shape[1]
    n_p = bsz * seq
    n_s = nb * steps
    n_all = n_p + n_s
    t_mix, cb, t_tok, tm = _tile_sizes(seq, nb)
    assert seq % t_mix == 0 and nb % cb == 0 and n_p % t_tok == 0 and n_s % t_tok == 0

    row = lambda a: a[0].reshape(1, -1)
    wts = dict(
        gmix=row(g_mix), win=w_in[0].astype(_BF), wca=w_conv_a[0], bca=row(b_conv_a),
        wra=w_rg_a[0].astype(_BF), bra=row(b_rg_a), wrx=w_rg_x[0].astype(_BF), brx=row(b_rg_x),
        lam=row(rg_lambda), wdw=w_conf_dw[0], bdw=row(b_conf_dw), gln=row(g_conf_ln), bln=row(b_conf_ln),
        wpw=w_conf_pw[0].astype(_BF), bpw=row(b_conf_pw), wout=w_out[0].astype(_BF), gffn=row(g_ffn),
        wrt=w_router[0].T.astype(_BF), br=b_router[0].reshape(ne, 1),
    )

    (h_p, xn_all, idx_p, prob_p, rank_p, cnt_p, pca, ph, pcf) = _prompt_mixer(x_prompt, n_all, wts, t_mix)
    to_cm = lambda a: jnp.swapaxes(a.reshape(nb // cb, cb, a.shape[1], d), 1, 2).reshape(-1, d)
    (h_s, xn_all, idx_s, prob_s, rank_s, cnt, v_s, sh_new, xa_s) = _sample_mixer(
        to_cm(x_sample), to_cm(state_conv_a[0]), state_h[0], to_cm(state_conf[0]), cnt_p, xn_all,
        wts, cb, steps)

    idx = jnp.concatenate([idx_p, idx_s], axis=1)
    rank = jnp.concatenate([rank_p, rank_s], axis=1)
    prob_t = jnp.concatenate([prob_p, prob_s], axis=1).T
    counts = cnt[:, 0].astype(jnp.int32)
    tiles_e = (counts + tm - 1) // tm
    tile_end = jnp.cumsum(tiles_e)
    offs = (tile_end - tiles_e) * tm
    n_tiles = (n_all * TOP_K) // tm + ne
    tile_ids = jnp.arange(n_tiles, dtype=jnp.int32)
    tile_expert = jnp.minimum(jnp.sum((tile_end[None, :] <= tile_ids[:, None]).astype(jnp.int32), axis=1), ne - 1)
    n_used = tile_end[-1:].astype(jnp.int32)

    xs = _dispatch(offs, idx, rank, xn_all, n_tiles * tm, t_tok)
    ys = _expert_ffn(tile_expert, n_used, xs, w_gate[0], b_gate[0].reshape(ne, 1, -1), w_up[0],
                     b_up[0].reshape(ne, 1, -1), w_down[0], b_down[0].reshape(ne, 1, -1), tm)
    gfin = g_final.reshape(1, d)
    y_p = _combine(offs, idx, rank, h_p.reshape(n_p, d), prob_t, gfin, ys, t_tok, 0)
    y_s = _combine(offs, idx, rank, h_s, prob_t, gfin, ys, t_tok, n_p // t_tok)

    nat = lambda a: jnp.swapaxes(a.reshape(nb // cb, steps, cb, d), 1, 2).reshape(nb, steps, d)
    y_prompt = y_p.reshape(bsz, seq, d)
    y_sample = nat(y_s)
    p_conv_a = pca[:, CONV_A_HALO - (kw - 1):, :][None]
    p_h = ph.reshape(1, bsz, d)
    p_conf = pcf[:, CONF_HALO - (kc - 1):, :][None]
    s_conv_a = jnp.concatenate([state_conv_a[0], nat(xa_s)], axis=1)[:, -(kw - 1):, :][None]
    s_h = sh_new[None]
    s_conf = jnp.concatenate([state_conf[0], nat(v_s)], axis=1)[:, -(kc - 1):, :][None]
    return (y_prompt, y_sample, p_conv_a, p_h, p_conf, s_conv_a, s_h, s_conf)
```

```python
import functools

import jax
import jax.numpy as jnp
from jax import lax
from jax.experimental import pallas as pl
from jax.experimental.pallas import tpu as pltpu

EPS = 1e-6
RG_C = 8.0
SWIGLU_LIMIT = 7.0
SWIGLU_ALPHA = 1.702
TOP_K = 4
LANE = 128
SUBLANE = 8
CONV_A_HALO = 8
CONF_HALO = 32
SEG_ALIGN = SUBLANE
VMEM_LIMIT = 56 * 1024 * 1024

_BF = jnp.bfloat16
_F32 = jnp.float32


def _const_spec(shape):
    nd = len(shape)
    return pl.BlockSpec(shape, lambda *_: (0,) * nd, pipeline_mode=pl.Buffered(1))


def _sorted_rows(tt, ne):
    return TOP_K * tt + ne * SEG_ALIGN


def _rms(x, g):
    return x * lax.rsqrt(jnp.mean(x * x, axis=-1, keepdims=True) + EPS) * g


def _sigmoid(x):
    return jax.nn.sigmoid(x)


def _dot(a, b):
    return jnp.dot(a, b, preferred_element_type=_F32)


def _block_diag(xb, w_ref, b_ref, out_ref):
    nb, gb, _ = w_ref.shape
    for n in range(nb):
        cs = slice(n * gb, (n + 1) * gb)
        out_ref[:, cs] = _dot(xb[:, cs], w_ref[n]) + b_ref[:, cs]


def _rglru_coeffs(xc_ref, r_ref, i_ref, lam_ref, a_ref, u_ref):
    sp = jax.nn.softplus(-lam_ref[...])
    r = _sigmoid(r_ref[...])
    i = _sigmoid(i_ref[...])
    log_a = -RG_C * r * sp
    a = jnp.exp(log_a)
    a_ref[...] = a
    u_ref[...] = jnp.sqrt(-jnp.tanh(log_a) * (a * a + 1.0)) * (i * xc_ref[...])


def _layernorm_silu(vc, g, b):
    mu = jnp.mean(vc, axis=-1, keepdims=True)
    xc = vc - mu
    y = xc * lax.rsqrt(jnp.mean(xc * xc, axis=-1, keepdims=True) + EPS) * g + b
    return y * _sigmoid(y)


def _route_and_sort(xn_b, wrt_ref, br_ref, pos_ref, prob_ref, cnt_ref, xs_ref):
    ne = wrt_ref.shape[0]
    tt = xn_b.shape[0]
    slr = xs_ref.shape[0]
    logits = lax.dot_general(wrt_ref[...], xn_b, (((1,), (1,)), ((), ())),
                             preferred_element_type=_F32) + br_ref[...]
    eidx = lax.broadcasted_iota(jnp.int32, (ne, tt), 0).astype(_F32)
    work = logits
    vals, onehots = [], []
    for _ in range(TOP_K):
        m = jnp.max(work, axis=0, keepdims=True)
        pick = jnp.min(jnp.where(work == m, eidx, float(ne)), axis=0, keepdims=True)
        one = eidx == pick
        work = jnp.where(one, -jnp.inf, work)
        vals.append(m)
        onehots.append(one)
    exps = [jnp.exp(v - vals[0]) for v in vals]
    inv = 1.0 / (exps[0] + exps[1] + exps[2] + exps[3])
    sel = jnp.zeros((ne, tt), _F32)
    for one in onehots:
        sel = sel + one.astype(_F32)
    row = lax.broadcasted_iota(jnp.int32, (tt, tt), 0)
    col = lax.broadcasted_iota(jnp.int32, (tt, tt), 1)
    incl = _dot(sel.astype(_BF), (row <= col).astype(_BF))
    count = incl[:, tt - 1:tt]
    groups = jnp.floor((count + (SEG_ALIGN - 1)) * (1.0 / SEG_ALIGN))
    er = lax.broadcasted_iota(jnp.int32, (ne, ne), 0)
    ec = lax.broadcasted_iota(jnp.int32, (ne, ne), 1)
    lo = _dot((ec < er).astype(_BF), jnp.broadcast_to(groups, (ne, LANE)).astype(_BF))[:, 0:1] * SEG_ALIGN
    before = lo + (incl - sel)
    riota = lax.broadcasted_iota(jnp.int32, (slr, tt), 0).astype(_F32)
    perm = jnp.zeros((slr, tt), _F32)
    for k in range(TOP_K):
        pos = jnp.sum(jnp.where(onehots[k], before, 0.0), axis=0, keepdims=True)
        pos_ref[k:k + 1, :] = pos
        prob_ref[k:k + 1, :] = exps[k] * inv
        perm = perm + jnp.where(riota == pos, 1.0, 0.0)
    cnt_ref[...] = jnp.broadcast_to(count, cnt_ref.shape)
    xs_ref[...] = _dot(perm.astype(_BF), xn_b)


def _tail(x, ya, yb, ga, gb, wout_ref, gffn_ref, h_ref):
    m = _sigmoid(ga) * ya + _sigmoid(gb) * yb
    h = x + _dot(m.astype(_BF), wout_ref[...])
    h_ref[...] = h
    return _rms(h, gffn_ref[...]).astype(_BF)


_W_NAMES = ["gmix", "win", "wca", "bca", "wra", "bra", "wrx", "brx", "lam", "wdw", "bdw", "gln", "bln",
            "wpw", "bpw", "wout", "gffn", "wrt", "br"]


def _causal_conv_tile(x, ext_ref, w_ref, b_ref, out_ref, first):
    tt, d = x.shape
    halo = ext_ref.shape[1] - tt
    kk = w_ref.shape[0]
    off = halo - (kk - 1)
    groups = 4
    rows_per_iter = groups * SUBLANE
    for c in range(d // LANE):
        cs = slice(c * LANE, (c + 1) * LANE)

        @pl.when(first)
        def _():
            ext_ref[c, 0:halo, :] = jnp.zeros((halo, LANE), _F32)

        ext_ref[c, halo:halo + tt, :] = x[:, cs]
        if kk <= SUBLANE:
            acc = b_ref[0:1, cs] + w_ref[0:1, cs] * ext_ref[c, off:off + tt, :]
            for k in range(1, kk):
                acc = acc + w_ref[k:k + 1, cs] * ext_ref[c, off + k:off + k + tt, :]
            out_ref[:, cs] = acc
            ext_ref[c, 0:halo, :] = ext_ref[c, tt:tt + halo, :]
            continue
        wb = [jnp.broadcast_to(w_ref[k:k + 1, cs], (SUBLANE, LANE)) for k in range(kk)]
        bb = jnp.broadcast_to(b_ref[0:1, cs], (SUBLANE, LANE))

        def blk(i, carry, c=c, cs=cs, wb=wb, bb=bb):
            r0 = pl.multiple_of(i * rows_per_iter, rows_per_iter)
            for q in range(groups):
                acc = bb
                for k in range(kk):
                    acc = acc + wb[k] * ext_ref[c, pl.ds(r0 + q * SUBLANE + off + k, SUBLANE), :]
                out_ref[pl.ds(r0 + q * SUBLANE, SUBLANE), cs] = acc
            return carry

        lax.fori_loop(0, tt // rows_per_iter, blk, 0)
        ext_ref[c, 0:halo, :] = ext_ref[c, tt:tt + halo, :]


def _prompt_mixer_body(x_ref, gmix_ref, win_ref, wca_ref, bca_ref, wra_ref, bra_ref, wrx_ref, brx_ref,
                       lam_ref, wdw_ref, bdw_ref, gln_ref, bln_ref, wpw_ref, bpw_ref, wout_ref,
                       gffn_ref, wrt_ref, br_ref,
                       h_ref, xs_ref, pos_ref, prob_ref, cnt_ref, pca_ref, ph_ref, pcf_ref,
                       xa_ext, v_ext, s0, s1, s2, s3, s4, hcar):
    tt, d = x_ref.shape
    j = pl.program_id(1)

    @pl.when(j == 0)
    def _():
        hcar[...] = jnp.zeros_like(hcar)

    x = x_ref[...]
    hn = _rms(x, gmix_ref[...]).astype(_BF)

    xa = _dot(hn, win_ref[:, 0:d])
    _causal_conv_tile(xa, xa_ext, wca_ref, bca_ref, s4, j == 0)
    pca_ref[...] = xa[tt - CONV_A_HALO:tt, :]
    xcb = s4[...].astype(_BF)
    _block_diag(xcb, wra_ref, bra_ref, s0)
    _block_diag(xcb, wrx_ref, brx_ref, s1)
    _rglru_coeffs(s4, s0, s1, lam_ref, s2, s3)

    ri = lax.broadcasted_iota(jnp.int32, (SUBLANE, d), 0)

    def scan_blk(i, h):
        r0 = pl.multiple_of(i * SUBLANE, SUBLANE)
        a = s2[pl.ds(r0, SUBLANE), :]
        u = s3[pl.ds(r0, SUBLANE), :]
        for s in (1, 2, 4):
            keep = ri >= s
            a_sh = jnp.where(keep, pltpu.roll(a, s, axis=0), 1.0)
            u_sh = jnp.where(keep, pltpu.roll(u, s, axis=0), 0.0)
            u = a * u_sh + u
            a = a * a_sh
        hh = a * h + u
        s0[pl.ds(r0, SUBLANE), :] = hh
        return jnp.broadcast_to(hh[SUBLANE - 1:SUBLANE, :], (SUBLANE, d))

    h_last = lax.fori_loop(0, tt // SUBLANE, scan_blk, hcar[...])
    hcar[...] = h_last
    ph_ref[...] = h_last[0:1, :]

    v = _dot(hn, win_ref[:, d:2 * d]) * _sigmoid(_dot(hn, win_ref[:, 2 * d:3 * d]))
    pcf_ref[...] = v[tt - CONF_HALO:tt, :]
    _causal_conv_tile(v, v_ext, wdw_ref, bdw_ref, s1, j == 0)
    sl = _layernorm_silu(s1[...], gln_ref[...], bln_ref[...])
    yb = _dot(sl.astype(_BF), wpw_ref[...]) + bpw_ref[...]

    ga = _dot(hn, win_ref[:, 3 * d:4 * d])
    gb = _dot(hn, win_ref[:, 4 * d:5 * d])
    xn_b = _tail(x, s0[...], yb, ga, gb, wout_ref, gffn_ref, h_ref)
    _route_and_sort(xn_b, wrt_ref, br_ref, pos_ref, prob_ref, cnt_ref, xs_ref)


def _routing_outputs(n_tiles, tt, d, ne):
    slr = _sorted_rows(tt, ne)
    shapes = [
        jax.ShapeDtypeStruct((n_tiles * slr, d), _F32),
        jax.ShapeDtypeStruct((TOP_K, n_tiles * tt), _F32),
        jax.ShapeDtypeStruct((TOP_K, n_tiles * tt), _F32),
        jax.ShapeDtypeStruct((n_tiles * ne, LANE), _F32),
    ]
    blocks = [(slr, d), (TOP_K, tt), (TOP_K, tt), (ne, LANE)]
    return shapes, blocks


def _prompt_mixer(x, wts, tt):
    bsz, t, d = x.shape
    nt = t // tt
    ne = wts["wrt"].shape[0]
    w_args = [wts[n] for n in _W_NAMES]
    r_shapes, r_blocks = _routing_outputs(bsz * nt, tt, d, ne)
    tile_major = lambda blk: pl.BlockSpec(blk, (lambda b, j: (b * nt + j, 0)) if blk[0] != TOP_K
                                          else (lambda b, j: (0, b * nt + j)))
    out_shape = [jax.ShapeDtypeStruct((bsz, t, d), _F32)] + r_shapes + [
        jax.ShapeDtypeStruct((bsz, CONV_A_HALO, d), _F32),
        jax.ShapeDtypeStruct((bsz, 1, d), _F32),
        jax.ShapeDtypeStruct((bsz, CONF_HALO, d), _F32),
    ]
    out_specs = [pl.BlockSpec((None, tt, d), lambda b, j: (b, j, 0))] + [tile_major(blk) for blk in r_blocks] + [
        pl.BlockSpec((None, CONV_A_HALO, d), lambda b, j: (b, 0, 0)),
        pl.BlockSpec((None, 1, d), lambda b, j: (b, 0, 0)),
        pl.BlockSpec((None, CONF_HALO, d), lambda b, j: (b, 0, 0)),
    ]
    scratch = [
        pltpu.VMEM((d // LANE, tt + CONV_A_HALO, LANE), _F32),
        pltpu.VMEM((d // LANE, tt + CONF_HALO, LANE), _F32),
        pltpu.VMEM((tt, d), _F32), pltpu.VMEM((tt, d), _F32), pltpu.VMEM((tt, d), _F32),
        pltpu.VMEM((tt, d), _F32), pltpu.VMEM((tt, d), _F32),
        pltpu.VMEM((SUBLANE, d), _F32),
    ]
    return pl.pallas_call(
        _prompt_mixer_body,
        grid=(bsz, nt),
        in_specs=[pl.BlockSpec((None, tt, d), lambda b, j: (b, j, 0))] + [_const_spec(w.shape) for w in w_args],
        out_specs=out_specs,
        out_shape=out_shape,
        scratch_shapes=scratch,
        compiler_params=pltpu.CompilerParams(dimension_semantics=("arbitrary", "arbitrary"),
                                             vmem_limit_bytes=VMEM_LIMIT),
        name="prompt_mixer",
    )(x, *w_args)


def _sample_mixer_body(nb, x_ref, sca_ref, sh_ref, scf_ref,
                       gmix_ref, win_ref, wca_ref, bca_ref, wra_ref, bra_ref, wrx_ref, brx_ref,
                       lam_ref, wdw_ref, bdw_ref, gln_ref, bln_ref, wpw_ref, bpw_ref, wout_ref,
                       gffn_ref, wrt_ref, br_ref,
                       h_ref, xs_ref, pos_ref, prob_ref, cnt_ref, v_out_ref, sh_out_ref, xa_out_ref,
                       s0, s1, s2, s3, s4):
    n, d = x_ref.shape
    steps = n // nb
    x = x_ref[...]
    hn = _rms(x, gmix_ref[...]).astype(_BF)

    xa = _dot(hn, win_ref[:, 0:d])
    xa_out_ref[...] = xa
    kw = wca_ref.shape[0]

    def xcat_a(slab):
        if slab < kw - 1:
            return sca_ref[slab * nb:(slab + 1) * nb, :]
        s = slab - (kw - 1)
        return xa[s * nb:(s + 1) * nb, :]

    for t in range(steps):
        acc = bca_ref[...] + jnp.zeros((nb, d), _F32)
        for k in range(kw):
            acc = acc + wca_ref[k:k + 1, :] * xcat_a(t + k)
        s4[t * nb:(t + 1) * nb, :] = acc
    xcb = s4[...].astype(_BF)
    _block_diag(xcb, wra_ref, bra_ref, s0)
    _block_diag(xcb, wrx_ref, brx_ref, s1)
    _rglru_coeffs(s4, s0, s1, lam_ref, s2, s3)
    h = sh_ref[...]
    for t in range(steps):
        h = s2[t * nb:(t + 1) * nb, :] * h + s3[t * nb:(t + 1) * nb, :]
        s0[t * nb:(t + 1) * nb, :] = h
    sh_out_ref[...] = h

    v = _dot(hn, win_ref[:, d:2 * d]) * _sigmoid(_dot(hn, win_ref[:, 2 * d:3 * d]))
    v_out_ref[...] = v
    kc = wdw_ref.shape[0]

    def xcat_v(slab):
        if slab < kc - 1:
            return scf_ref[slab * nb:(slab + 1) * nb, :]
        s = slab - (kc - 1)
        return v[s * nb:(s + 1) * nb, :]

    for t in range(steps):
        acc = bdw_ref[...] + jnp.zeros((nb, d), _F32)
        for k in range(kc):
            acc = acc + wdw_ref[k:k + 1, :] * xcat_v(t + k)
        s1[t * nb:(t + 1) * nb, :] = acc
    sl = _layernorm_silu(s1[...], gln_ref[...], bln_ref[...])
    yb = _dot(sl.astype(_BF), wpw_ref[...]) + bpw_ref[...]

    ga = _dot(hn, win_ref[:, 3 * d:4 * d])
    gb = _dot(hn, win_ref[:, 4 * d:5 * d])
    xn_b = _tail(x, s0[...], yb, ga, gb, wout_ref, gffn_ref, h_ref)
    _route_and_sort(xn_b, wrt_ref, br_ref, pos_ref, prob_ref, cnt_ref, xs_ref)


def _sample_mixer(x_cm, sca_cm, sh, scf_cm, wts, cb, steps):
    n, d = x_cm.shape
    rows = cb * steps
    n_chunks = n // rows
    ne = wts["wrt"].shape[0]
    kw = wts["wca"].shape[0]
    kc = wts["wdw"].shape[0]
    w_args = [wts[n_] for n_ in _W_NAMES]
    chunk = lambda r: pl.BlockSpec((r, d), lambda i: (i, 0))
    r_shapes, r_blocks = _routing_outputs(n_chunks, rows, d, ne)
    tile_major = lambda blk: pl.BlockSpec(blk, (lambda i: (i, 0)) if blk[0] != TOP_K else (lambda i: (0, i)))
    conf_state = pl.BlockSpec(((kc - 1) * cb, d), lambda i: (i, 0), pipeline_mode=pl.Buffered(1))
    in_specs = [chunk(rows), chunk((kw - 1) * cb), chunk(cb), conf_state] + \
        [_const_spec(w.shape) for w in w_args]
    out_shape = [jax.ShapeDtypeStruct((n, d), _F32)] + r_shapes + [
        jax.ShapeDtypeStruct((n, d), _F32),
        jax.ShapeDtypeStruct(sh.shape, _F32),
        jax.ShapeDtypeStruct((n, d), _F32),
    ]
    out_specs = [chunk(rows)] + [tile_major(blk) for blk in r_blocks] + [chunk(rows), chunk(cb), chunk(rows)]
    scratch = [pltpu.VMEM((rows, d), _F32)] * 5
    return pl.pallas_call(
        functools.partial(_sample_mixer_body, cb),
        grid=(n_chunks,),
        in_specs=in_specs,
        out_specs=out_specs,
        out_shape=out_shape,
        scratch_shapes=scratch,
        compiler_params=pltpu.CompilerParams(dimension_semantics=("arbitrary",),
                                             vmem_limit_bytes=VMEM_LIMIT),
        name="sample_mixer",
    )(x_cm, sca_cm, sh, scf_cm, *w_args)


def _expert_ffn_body(ntp, slr, ne,
                     te_ref, nu_ref, r0_ref, valid_ref, ilo_ref, ihi_ref, base_ref, cp_ref, lo_ref, used_ref,
                     xsp_ref, xss_ref, wg_ref, bg_ref, wu_ref, bu_ref, wd_ref, bd_ref,
                     ysp_ref, yss_ref,
                     lhs, obuf, zbuf, wg_b, wu_b, wd_b, gsem, ssem, zsem):
    j = pl.program_id(0)
    nu = nu_ref[0]
    nt_all = used_ref.shape[0]

    def for_pieces(step, fn):
        e = te_ref[step]
        r0 = r0_ref[step]
        r1 = r0 + valid_ref[step]

        def body(i, c):
            s0 = base_ref[i * ne + e]
            a = jnp.maximum(s0, r0)
            b = jnp.minimum(s0 + cp_ref[i * ne + e], r1)

            @pl.when(b > a)
            def _():
                n = pl.multiple_of(b - a, SEG_ALIGN)
                loc = lo_ref[i * ne + e] + (a - s0)
                off = pl.multiple_of(a - r0, SEG_ALIGN)

                @pl.when(i < ntp)
                def _():
                    fn(0, pl.multiple_of(i * slr + loc, SEG_ALIGN), off, n)

                @pl.when(i >= ntp)
                def _():
                    fn(1, pl.multiple_of((i - ntp) * slr + loc, SEG_ALIGN), off, n)

            return c

        lax.fori_loop(ilo_ref[step], ihi_ref[step], body, 0)

    def gather_start(step, slot):
        def fn(group, row, off, n):
            src = (xsp_ref, xss_ref)[group]
            pltpu.make_async_copy(src.at[pl.ds(row, n), :], lhs.at[slot, pl.ds(off, n), :], gsem.at[slot]).start()
        for_pieces(step, fn)

    def scatter_start(step, slot):
        def fn(group, row, off, n):
            dst = (ysp_ref, yss_ref)[group]
            pltpu.make_async_copy(obuf.at[slot, pl.ds(off, n), :], dst.at[pl.ds(row, n), :], ssem.at[slot]).start()
        for_pieces(step, fn)

    def gather_wait(step, slot):
        n = pl.multiple_of(valid_ref[step], SEG_ALIGN)
        pltpu.make_async_copy(xsp_ref.at[pl.ds(0, n), :], lhs.at[slot, pl.ds(0, n), :], gsem.at[slot]).wait()

    def scatter_wait(step, slot):
        n = pl.multiple_of(valid_ref[step], SEG_ALIGN)
        pltpu.make_async_copy(obuf.at[slot, pl.ds(0, n), :], ysp_ref.at[pl.ds(0, n), :], ssem.at[slot]).wait()

    def zero_tail(i, do_wait):
        used = pl.multiple_of(used_ref[i], SEG_ALIGN)
        n = pl.multiple_of(slr - used, SEG_ALIGN)

        def go(dst, row):
            cp = pltpu.make_async_copy(zbuf.at[pl.ds(0, n), :], dst.at[pl.ds(row, n), :], zsem)
            cp.wait() if do_wait else cp.start()

        @pl.when((n > 0) & (i < ntp))
        def _():
            go(ysp_ref, pl.multiple_of(i * slr + used, SEG_ALIGN))

        @pl.when((n > 0) & (i >= ntp))
        def _():
            go(yss_ref, pl.multiple_of((i - ntp) * slr + used, SEG_ALIGN))

    @pl.when(j == 0)
    def _():
        lhs[...] = jnp.zeros_like(lhs)
        zbuf[...] = jnp.zeros_like(zbuf)
        lax.fori_loop(0, nt_all, lambda i, c: (zero_tail(i, False), c)[1], 0)
        lax.fori_loop(0, nt_all, lambda i, c: (zero_tail(i, True), c)[1], 0)
        gather_start(0, 0)

    @pl.when(j + 1 < nu)
    def _():
        gather_start(j + 1, (j + 1) % 2)

    slot = j % 2

    @pl.when((j >= 2) & (j - 2 < nu))
    def _():
        scatter_wait(j - 2, slot)

    @pl.when(j < nu)
    def _():
        gather_wait(j, slot)
        prev = te_ref[jnp.maximum(j - 1, 0)]

        @pl.when((j == 0) | (te_ref[j] != prev))
        def _():
            wg_b[...] = wg_ref[...].astype(_BF)
            wu_b[...] = wu_ref[...].astype(_BF)
            wd_b[...] = wd_ref[...].astype(_BF)

        x = lhs[slot].astype(_BF)
        g = jnp.minimum(_dot(x, wg_b[...]) + bg_ref[...], SWIGLU_LIMIT)
        u = jnp.clip(_dot(x, wu_b[...]) + bu_ref[...], -SWIGLU_LIMIT, SWIGLU_LIMIT)
        hdn = (u + 1.0) * (g * _sigmoid(SWIGLU_ALPHA * g))
        obuf[slot] = _dot(hdn.astype(_BF), wd_b[...]) + bd_ref[...]
        scatter_start(j, slot)


def _expert_ffn(meta, xs_p, xs_s, wg, bg, wu, bu, wd, bd, tm, slr, n_tiles):
    ne, d, de = wg.shape
    ntp = xs_p.shape[0] // slr
    n_meta = len(meta)
    wspec = lambda a, b_: pl.BlockSpec((None, a, b_), lambda j, te, *_: (te[j], 0, 0))
    hbm = pl.BlockSpec(memory_space=pl.ANY)
    tail_rows = ne * SEG_ALIGN
    return pl.pallas_call(
        functools.partial(_expert_ffn_body, ntp, slr, ne),
        grid_spec=pltpu.PrefetchScalarGridSpec(
            num_scalar_prefetch=n_meta,
            grid=(n_tiles,),
            in_specs=[hbm, hbm, wspec(d, de), wspec(1, de), wspec(d, de), wspec(1, de), wspec(de, d), wspec(1, d)],
            out_specs=[hbm, hbm],
            scratch_shapes=[
                pltpu.VMEM((2, tm, d), _F32), pltpu.VMEM((2, tm, d), _F32),
                pltpu.VMEM((tail_rows, d), _F32),
                pltpu.VMEM((d, de), _BF), pltpu.VMEM((d, de), _BF), pltpu.VMEM((de, d), _BF),
                pltpu.SemaphoreType.DMA((2,)), pltpu.SemaphoreType.DMA((2,)), pltpu.SemaphoreType.DMA(()),
            ],
        ),
        out_shape=[jax.ShapeDtypeStruct(xs_p.shape, _F32), jax.ShapeDtypeStruct(xs_s.shape, _F32)],
        compiler_params=pltpu.CompilerParams(dimension_semantics=("arbitrary",),
                                             vmem_limit_bytes=VMEM_LIMIT),
        name="expert_ffn",
    )(*meta, xs_p, xs_s, wg, bg, wu, bu, wd, bd)


def _combine_body(ys_ref, h_ref, pos_ref, prob_ref, gfin_ref, y_ref):
    tt, d = h_ref.shape
    slr = ys_ref.shape[0]
    ciota = lax.broadcasted_iota(jnp.int32, (tt, slr), 1).astype(_F32)
    w = jnp.zeros((tt, slr), _F32)
    for k in range(TOP_K):
        w = w + jnp.where(ciota == pos_ref[:, k:k + 1], prob_ref[:, k:k + 1], 0.0)
    out = h_ref[...] + _dot(w.astype(_BF), ys_ref[...].astype(_BF))
    y_ref[...] = _rms(out, gfin_ref[...])


def _combine(ys, h, pos_t, prob_t, g_final, tt, slr):
    n, d = h.shape
    return pl.pallas_call(
        _combine_body,
        grid=(n // tt,),
        in_specs=[pl.BlockSpec((slr, d), lambda i: (i, 0)),
                  pl.BlockSpec((tt, d), lambda i: (i, 0)),
                  pl.BlockSpec((tt, TOP_K), lambda i: (i, 0)),
                  pl.BlockSpec((tt, TOP_K), lambda i: (i, 0)),
                  pl.BlockSpec((1, d), lambda i: (0, 0))],
        out_specs=pl.BlockSpec((tt, d), lambda i: (i, 0)),
        out_shape=jax.ShapeDtypeStruct((n, d), _F32),
        compiler_params=pltpu.CompilerParams(dimension_semantics=("arbitrary",),
                                             vmem_limit_bytes=VMEM_LIMIT),
        name="combine",
    )(ys, h, pos_t, prob_t, g_final)


def _tile_sizes(seq, nb, steps):
    tt = min(256, seq)
    return tt, tt // steps, 512


def _row_tile_tables(counts, tm, n_tiles):
    nt_all, ne = counts.shape
    cp = (counts + (SEG_ALIGN - 1)) // SEG_ALIGN * SEG_ALIGN
    lo = jnp.cumsum(cp, axis=1) - cp
    base = jnp.cumsum(cp, axis=0) - cp
    used = jnp.sum(cp, axis=1)
    total = jnp.sum(cp, axis=0)
    tiles_e = (total + tm - 1) // tm
    tile_end = jnp.cumsum(tiles_e)
    first_tile = tile_end - tiles_e
    ids = jnp.arange(n_tiles, dtype=jnp.int32)
    te = jnp.minimum(jnp.sum((tile_end[None, :] <= ids[:, None]).astype(jnp.int32), axis=1), ne - 1)
    r0 = (ids - first_tile[te]) * tm
    valid = jnp.clip(total[te] - r0, 0, tm)
    seg_start = base.T[te]
    seg_end = seg_start + cp.T[te]
    ilo = jnp.sum((seg_end <= r0[:, None]).astype(jnp.int32), axis=1)
    ihi = jnp.sum((seg_start < (r0 + valid)[:, None]).astype(jnp.int32), axis=1)
    i32 = lambda a: a.astype(jnp.int32)
    return [i32(te), i32(tile_end[-1:]), i32(r0), i32(valid), i32(ilo), i32(ihi),
            i32(base.reshape(-1)), i32(cp.reshape(-1)), i32(lo.reshape(-1)), i32(used)]


def kernel(x_prompt, x_sample, state_conv_a, state_h, state_conf, g_mix, w_in, w_conv_a, b_conv_a,
           w_rg_a, b_rg_a, w_rg_x, b_rg_x, rg_lambda, w_conf_dw, b_conf_dw, g_conf_ln, b_conf_ln,
           w_conf_pw, b_conf_pw, w_out, g_ffn, w_router, b_router, w_gate, b_gate, w_up, b_up,
           w_down, b_down, g_final):
    depth = g_mix.shape[0]
    assert depth == 1
    bsz, seq, d = x_prompt.shape
    nb, steps, _ = x_sample.shape
    ne = w_router.shape[-1]
    kw = w_conv_a.shape[1]
    kc = w_conf_dw.shape[1]
    n_p = bsz * seq
    n_s = nb * steps
    tt, cb, tm = _tile_sizes(seq, nb, steps)
    assert seq % tt == 0 and nb % cb == 0 and cb * steps == tt
    slr = _sorted_rows(tt, ne)
    ntp, nts = n_p // tt, n_s // tt

    row = lambda a: a[0].reshape(1, -1)
    wts = dict(
        gmix=row(g_mix), win=w_in[0].astype(_BF), wca=w_conv_a[0], bca=row(b_conv_a),
        wra=w_rg_a[0].astype(_BF), bra=row(b_rg_a), wrx=w_rg_x[0].astype(_BF), brx=row(b_rg_x),
        lam=row(rg_lambda), wdw=w_conf_dw[0], bdw=row(b_conf_dw), gln=row(g_conf_ln), bln=row(b_conf_ln),
        wpw=w_conf_pw[0].astype(_BF), bpw=row(b_conf_pw), wout=w_out[0].astype(_BF), gffn=row(g_ffn),
        wrt=w_router[0].T.astype(_BF), br=b_router[0].reshape(ne, 1),
    )

    (h_p, xs_p, pos_p, prob_p, cnt_p, pca, ph, pcf) = _prompt_mixer(x_prompt, wts, tt)
    to_cm = lambda a: jnp.swapaxes(a.reshape(nb // cb, cb, a.shape[1], d), 1, 2).reshape(-1, d)
    (h_s, xs_s, pos_s, prob_s, cnt_s, v_s, sh_new, xa_s) = _sample_mixer(
        to_cm(x_sample), to_cm(state_conv_a[0]), state_h[0], to_cm(state_conf[0]), wts, cb, steps)

    counts = jnp.concatenate([cnt_p[:, 0].reshape(ntp, ne), cnt_s[:, 0].reshape(nts, ne)], axis=0)
    n_tiles = ((ntp + nts) * slr) // tm + ne + 2
    meta = _row_tile_tables(counts.astype(jnp.int32), tm, n_tiles)

    ys_p, ys_s = _expert_ffn(meta, xs_p, xs_s, w_gate[0], b_gate[0].reshape(ne, 1, -1), w_up[0],
                             b_up[0].reshape(ne, 1, -1), w_down[0], b_down[0].reshape(ne, 1, -1), tm, slr, n_tiles)
    gfin = g_final.reshape(1, d)
    y_p = _combine(ys_p, h_p.reshape(n_p, d), pos_p.T, prob_p.T, gfin, tt, slr)
    y_s = _combine(ys_s, h_s, pos_s.T, prob_s.T, gfin, tt, slr)

    nat = lambda a: jnp.swapaxes(a.reshape(nb // cb, steps, cb, d), 1, 2).reshape(nb, steps, d)
    y_prompt = y_p.reshape(bsz, seq, d)
    y_sample = nat(y_s)
    p_conv_a = pca[:, CONV_A_HALO - (kw - 1):, :][None]
    p_h = ph.reshape(1, bsz, d)
    p_conf = pcf[:, CONF_HALO - (kc - 1):, :][None]
    s_conv_a = jnp.concatenate([state_conv_a[0], nat(xa_s)], axis=1)[:, -(kw - 1):, :][None]
    s_h = sh_new[None]
    s_conf = jnp.concatenate([state_conf[0], nat(v_s)], axis=1)[:, -(kc - 1):, :][None]
    return (y_prompt, y_sample, p_conv_a, p_h, p_conf, s_conv_a, s_h, s_conf)
```

```python
import functools

import jax
import jax.numpy as jnp
from jax import lax
from jax.experimental import pallas as pl
from jax.experimental.pallas import tpu as pltpu

EPS = 1e-6
RG_C = 8.0
SWIGLU_LIMIT = 7.0
SWIGLU_ALPHA = 1.702
TOP_K = 4
LANE = 128
SUBLANE = 8
CONV_A_HALO = 8
CONF_HALO = 32
TAP_GROUP = 8
SEG_ALIGN = SUBLANE
VMEM_LIMIT = 56 * 1024 * 1024

_BF = jnp.bfloat16
_F32 = jnp.float32


def _const_spec(shape):
    nd = len(shape)
    return pl.BlockSpec(shape, lambda *_: (0,) * nd, pipeline_mode=pl.Buffered(1))


def _sorted_rows(tt, ne):
    return TOP_K * tt + ne * SEG_ALIGN


def _rms(x, g):
    return x * lax.rsqrt(jnp.mean(x * x, axis=-1, keepdims=True) + EPS) * g


def _sigmoid(x):
    return jax.nn.sigmoid(x)


def _dot(a, b):
    return jnp.dot(a, b, preferred_element_type=_F32)


def _block_diag(xb, w_ref, b_ref, out_ref):
    nb, gb, _ = w_ref.shape
    for n in range(nb):
        cs = slice(n * gb, (n + 1) * gb)
        out_ref[:, cs] = _dot(xb[:, cs], w_ref[n]) + b_ref[:, cs]


def _rglru_coeffs(xc_ref, r_ref, i_ref, lam_ref, a_ref, u_ref):
    sp = jax.nn.softplus(-lam_ref[...])
    r = _sigmoid(r_ref[...])
    i = _sigmoid(i_ref[...])
    log_a = -RG_C * r * sp
    a = jnp.exp(log_a)
    a_ref[...] = a
    u_ref[...] = jnp.sqrt(-jnp.tanh(log_a) * (a * a + 1.0)) * (i * xc_ref[...])


def _layernorm_silu(vc, g, b):
    mu = jnp.mean(vc, axis=-1, keepdims=True)
    xc = vc - mu
    y = xc * lax.rsqrt(jnp.mean(xc * xc, axis=-1, keepdims=True) + EPS) * g + b
    return y * _sigmoid(y)


def _route_and_sort(xn_b, wrt_ref, br_ref, pos_ref, prob_ref, cnt_ref, xs_ref):
    ne = wrt_ref.shape[0]
    tt = xn_b.shape[0]
    slr = xs_ref.shape[0]
    logits = lax.dot_general(wrt_ref[...], xn_b, (((1,), (1,)), ((), ())),
                             preferred_element_type=_F32) + br_ref[...]
    eidx = lax.broadcasted_iota(jnp.int32, (ne, tt), 0).astype(_F32)
    work = logits
    vals, onehots = [], []
    for _ in range(TOP_K):
        m = jnp.max(work, axis=0, keepdims=True)
        pick = jnp.min(jnp.where(work == m, eidx, float(ne)), axis=0, keepdims=True)
        one = eidx == pick
        work = jnp.where(one, -jnp.inf, work)
        vals.append(m)
        onehots.append(one)
    exps = [jnp.exp(v - vals[0]) for v in vals]
    inv = 1.0 / (exps[0] + exps[1] + exps[2] + exps[3])
    sel = jnp.zeros((ne, tt), _F32)
    for one in onehots:
        sel = sel + one.astype(_F32)
    row = lax.broadcasted_iota(jnp.int32, (tt, tt), 0)
    col = lax.broadcasted_iota(jnp.int32, (tt, tt), 1)
    incl = _dot(sel.astype(_BF), (row <= col).astype(_BF))
    count = incl[:, tt - 1:tt]
    groups = jnp.floor((count + (SEG_ALIGN - 1)) * (1.0 / SEG_ALIGN))
    er = lax.broadcasted_iota(jnp.int32, (ne, ne), 0)
    ec = lax.broadcasted_iota(jnp.int32, (ne, ne), 1)
    lo = _dot((ec < er).astype(_BF), jnp.broadcast_to(groups, (ne, LANE)).astype(_BF))[:, 0:1] * SEG_ALIGN
    before = lo + (incl - sel)
    riota = lax.broadcasted_iota(jnp.int32, (slr, tt), 0).astype(_F32)
    perm = jnp.zeros((slr, tt), _F32)
    for k in range(TOP_K):
        pos = jnp.sum(jnp.where(onehots[k], before, 0.0), axis=0, keepdims=True)
        pos_ref[k:k + 1, :] = pos
        prob_ref[k:k + 1, :] = exps[k] * inv
        perm = perm + jnp.where(riota == pos, 1.0, 0.0)
    cnt_ref[...] = jnp.broadcast_to(count, cnt_ref.shape)
    xs_ref[...] = _dot(perm.astype(_BF), xn_b)


def _tail(x, ya, yb, ga, gb, wout_ref, gffn_ref, h_ref):
    m = _sigmoid(ga) * ya + _sigmoid(gb) * yb
    h = x + _dot(m.astype(_BF), wout_ref[...])
    h_ref[...] = h
    return _rms(h, gffn_ref[...]).astype(_BF)


_W_NAMES = ["gmix", "win", "wca", "bca", "wra", "bra", "wrx", "brx", "lam", "wdw", "bdw", "gln", "bln",
            "wpw", "bpw", "wout", "gffn", "wrt", "br"]


def _causal_conv_tile(x, ext_ref, w_ref, b_ref, out_ref):
    tt, d = x.shape
    halo = ext_ref.shape[1] - tt
    kk = w_ref.shape[0]
    off = halo - (kk - 1)
    for c in range(d // LANE):
        cs = slice(c * LANE, (c + 1) * LANE)
        ext_ref[c, halo:halo + tt, :] = x[:, cs]
        bb = jnp.broadcast_to(b_ref[0:1, cs], (SUBLANE, LANE))
        for k0 in range(0, kk, TAP_GROUP):
            taps = range(k0, min(k0 + TAP_GROUP, kk))
            wb = {k: jnp.broadcast_to(w_ref[k:k + 1, cs], (SUBLANE, LANE)) for k in taps}
            for r0 in range(0, tt, SUBLANE):
                acc = bb if k0 == 0 else out_ref[r0:r0 + SUBLANE, cs]
                for k in taps:
                    acc = acc + wb[k] * ext_ref[c, r0 + off + k:r0 + off + k + SUBLANE, :]
                out_ref[r0:r0 + SUBLANE, cs] = acc
        ext_ref[c, 0:halo, :] = ext_ref[c, tt:tt + halo, :]


def _prompt_mixer_body(x_ref, gmix_ref, win_ref, wca_ref, bca_ref, wra_ref, bra_ref, wrx_ref, brx_ref,
                       lam_ref, wdw_ref, bdw_ref, gln_ref, bln_ref, wpw_ref, bpw_ref, wout_ref,
                       gffn_ref, wrt_ref, br_ref,
                       h_ref, xs_ref, pos_ref, prob_ref, cnt_ref, pca_ref, ph_ref, pcf_ref,
                       xa_ext, v_ext, s0, s1, s2, s3, s4, hcar):
    tt, d = x_ref.shape
    j = pl.program_id(1)

    @pl.when(j == 0)
    def _():
        hcar[...] = jnp.zeros_like(hcar)
        xa_ext[:, 0:CONV_A_HALO, :] = jnp.zeros((d // LANE, CONV_A_HALO, LANE), _F32)
        v_ext[:, 0:CONF_HALO, :] = jnp.zeros((d // LANE, CONF_HALO, LANE), _F32)

    x = x_ref[...]
    hn = _rms(x, gmix_ref[...]).astype(_BF)

    xa = _dot(hn, win_ref[:, 0:d])
    _causal_conv_tile(xa, xa_ext, wca_ref, bca_ref, s4)
    pca_ref[...] = xa[tt - CONV_A_HALO:tt, :]
    xcb = s4[...].astype(_BF)
    _block_diag(xcb, wra_ref, bra_ref, s0)
    _block_diag(xcb, wrx_ref, brx_ref, s1)
    _rglru_coeffs(s4, s0, s1, lam_ref, s2, s3)

    ri = lax.broadcasted_iota(jnp.int32, (SUBLANE, d), 0)

    h_last = hcar[...]
    for r0 in range(0, tt, SUBLANE):
        a = s2[r0:r0 + SUBLANE, :]
        u = s3[r0:r0 + SUBLANE, :]
        for s in (1, 2, 4):
            keep = ri >= s
            a_sh = jnp.where(keep, pltpu.roll(a, s, axis=0), 1.0)
            u_sh = jnp.where(keep, pltpu.roll(u, s, axis=0), 0.0)
            u = a * u_sh + u
            a = a * a_sh
        hh = a * h_last + u
        s0[r0:r0 + SUBLANE, :] = hh
        h_last = jnp.broadcast_to(hh[SUBLANE - 1:SUBLANE, :], (SUBLANE, d))
    hcar[...] = h_last
    ph_ref[...] = h_last[0:1, :]

    v = _dot(hn, win_ref[:, d:2 * d]) * _sigmoid(_dot(hn, win_ref[:, 2 * d:3 * d]))
    pcf_ref[...] = v[tt - CONF_HALO:tt, :]
    _causal_conv_tile(v, v_ext, wdw_ref, bdw_ref, s1)
    sl = _layernorm_silu(s1[...], gln_ref[...], bln_ref[...])
    yb = _dot(sl.astype(_BF), wpw_ref[...]) + bpw_ref[...]

    ga = _dot(hn, win_ref[:, 3 * d:4 * d])
    gb = _dot(hn, win_ref[:, 4 * d:5 * d])
    xn_b = _tail(x, s0[...], yb, ga, gb, wout_ref, gffn_ref, h_ref)
    _route_and_sort(xn_b, wrt_ref, br_ref, pos_ref, prob_ref, cnt_ref, xs_ref)


def _routing_outputs(n_tiles, tt, d, ne):
    slr = _sorted_rows(tt, ne)
    shapes = [
        jax.ShapeDtypeStruct((n_tiles * slr, d), _F32),
        jax.ShapeDtypeStruct((TOP_K, n_tiles * tt), _F32),
        jax.ShapeDtypeStruct((TOP_K, n_tiles * tt), _F32),
        jax.ShapeDtypeStruct((n_tiles * ne, LANE), _F32),
    ]
    blocks = [(slr, d), (TOP_K, tt), (TOP_K, tt), (ne, LANE)]
    return shapes, blocks


def _prompt_mixer(x, wts, tt):
    bsz, t, d = x.shape
    nt = t // tt
    ne = wts["wrt"].shape[0]
    w_args = [wts[n] for n in _W_NAMES]
    r_shapes, r_blocks = _routing_outputs(bsz * nt, tt, d, ne)
    tile_major = lambda blk: pl.BlockSpec(blk, (lambda b, j: (b * nt + j, 0)) if blk[0] != TOP_K
                                          else (lambda b, j: (0, b * nt + j)))
    out_shape = [jax.ShapeDtypeStruct((bsz, t, d), _F32)] + r_shapes + [
        jax.ShapeDtypeStruct((bsz, CONV_A_HALO, d), _F32),
        jax.ShapeDtypeStruct((bsz, 1, d), _F32),
        jax.ShapeDtypeStruct((bsz, CONF_HALO, d), _F32),
    ]
    out_specs = [pl.BlockSpec((None, tt, d), lambda b, j: (b, j, 0))] + [tile_major(blk) for blk in r_blocks] + [
        pl.BlockSpec((None, CONV_A_HALO, d), lambda b, j: (b, 0, 0)),
        pl.BlockSpec((None, 1, d), lambda b, j: (b, 0, 0)),
        pl.BlockSpec((None, CONF_HALO, d), lambda b, j: (b, 0, 0)),
    ]
    scratch = [
        pltpu.VMEM((d // LANE, tt + CONV_A_HALO, LANE), _F32),
        pltpu.VMEM((d // LANE, tt + CONF_HALO, LANE), _F32),
        pltpu.VMEM((tt, d), _F32), pltpu.VMEM((tt, d), _F32), pltpu.VMEM((tt, d), _F32),
        pltpu.VMEM((tt, d), _F32), pltpu.VMEM((tt, d), _F32),
        pltpu.VMEM((SUBLANE, d), _F32),
    ]
    return pl.pallas_call(
        _prompt_mixer_body,
        grid=(bsz, nt),
        in_specs=[pl.BlockSpec((None, tt, d), lambda b, j: (b, j, 0))] + [_const_spec(w.shape) for w in w_args],
        out_specs=out_specs,
        out_shape=out_shape,
        scratch_shapes=scratch,
        compiler_params=pltpu.CompilerParams(dimension_semantics=("arbitrary", "arbitrary"),
                                             vmem_limit_bytes=VMEM_LIMIT),
        name="prompt_mixer",
    )(x, *w_args)


def _sample_mixer_body(nb, x_ref, sca_ref, sh_ref, scf_ref,
                       gmix_ref, win_ref, wca_ref, bca_ref, wra_ref, bra_ref, wrx_ref, brx_ref,
                       lam_ref, wdw_ref, bdw_ref, gln_ref, bln_ref, wpw_ref, bpw_ref, wout_ref,
                       gffn_ref, wrt_ref, br_ref,
                       h_ref, xs_ref, pos_ref, prob_ref, cnt_ref, v_out_ref, sh_out_ref, xa_out_ref,
                       s0, s1, s2, s3, s4):
    n, d = x_ref.shape
    steps = n // nb
    x = x_ref[...]
    hn = _rms(x, gmix_ref[...]).astype(_BF)

    xa = _dot(hn, win_ref[:, 0:d])
    xa_out_ref[...] = xa
    kw = wca_ref.shape[0]

    def xcat_a(slab):
        if slab < kw - 1:
            return sca_ref[slab * nb:(slab + 1) * nb, :]
        s = slab - (kw - 1)
        return xa[s * nb:(s + 1) * nb, :]

    for t in range(steps):
        acc = bca_ref[...] + jnp.zeros((nb, d), _F32)
        for k in range(kw):
            acc = acc + wca_ref[k:k + 1, :] * xcat_a(t + k)
        s4[t * nb:(t + 1) * nb, :] = acc
    xcb = s4[...].astype(_BF)
    _block_diag(xcb, wra_ref, bra_ref, s0)
    _block_diag(xcb, wrx_ref, brx_ref, s1)
    _rglru_coeffs(s4, s0, s1, lam_ref, s2, s3)
    h = sh_ref[...]
    for t in range(steps):
        h = s2[t * nb:(t + 1) * nb, :] * h + s3[t * nb:(t + 1) * nb, :]
        s0[t * nb:(t + 1) * nb, :] = h
    sh_out_ref[...] = h

    v = _dot(hn, win_ref[:, d:2 * d]) * _sigmoid(_dot(hn, win_ref[:, 2 * d:3 * d]))
    v_out_ref[...] = v
    kc = wdw_ref.shape[0]

    def xcat_v(slab):
        if slab < kc - 1:
            return scf_ref[slab * nb:(slab + 1) * nb, :]
        s = slab - (kc - 1)
        return v[s * nb:(s + 1) * nb, :]

    for t in range(steps):
        acc = bdw_ref[...] + jnp.zeros((nb, d), _F32)
        for k in range(kc):
            acc = acc + wdw_ref[k:k + 1, :] * xcat_v(t + k)
        s1[t * nb:(t + 1) * nb, :] = acc
    sl = _layernorm_silu(s1[...], gln_ref[...], bln_ref[...])
    yb = _dot(sl.astype(_BF), wpw_ref[...]) + bpw_ref[...]

    ga = _dot(hn, win_ref[:, 3 * d:4 * d])
    gb = _dot(hn, win_ref[:, 4 * d:5 * d])
    xn_b = _tail(x, s0[...], yb, ga, gb, wout_ref, gffn_ref, h_ref)
    _route_and_sort(xn_b, wrt_ref, br_ref, pos_ref, prob_ref, cnt_ref, xs_ref)


def _sample_mixer(x_cm, sca_cm, sh, scf_cm, wts, cb, steps):
    n, d = x_cm.shape
    rows = cb * steps
    n_chunks = n // rows
    ne = wts["wrt"].shape[0]
    kw = wts["wca"].shape[0]
    kc = wts["wdw"].shape[0]
    w_args = [wts[n_] for n_ in _W_NAMES]
    chunk = lambda r: pl.BlockSpec((r, d), lambda i: (i, 0))
    r_shapes, r_blocks = _routing_outputs(n_chunks, rows, d, ne)
    tile_major = lambda blk: pl.BlockSpec(blk, (lambda i: (i, 0)) if blk[0] != TOP_K else (lambda i: (0, i)))
    conf_state = pl.BlockSpec(((kc - 1) * cb, d), lambda i: (i, 0), pipeline_mode=pl.Buffered(1))
    in_specs = [chunk(rows), chunk((kw - 1) * cb), chunk(cb), conf_state] + \
        [_const_spec(w.shape) for w in w_args]
    out_shape = [jax.ShapeDtypeStruct((n, d), _F32)] + r_shapes + [
        jax.ShapeDtypeStruct((n, d), _F32),
        jax.ShapeDtypeStruct(sh.shape, _F32),
        jax.ShapeDtypeStruct((n, d), _F32),
    ]
    out_specs = [chunk(rows)] + [tile_major(blk) for blk in r_blocks] + [chunk(rows), chunk(cb), chunk(rows)]
    scratch = [pltpu.VMEM((rows, d), _F32)] * 5
    return pl.pallas_call(
        functools.partial(_sample_mixer_body, cb),
        grid=(n_chunks,),
        in_specs=in_specs,
        out_specs=out_specs,
        out_shape=out_shape,
        scratch_shapes=scratch,
        compiler_params=pltpu.CompilerParams(dimension_semantics=("arbitrary",),
                                             vmem_limit_bytes=VMEM_LIMIT),
        name="sample_mixer",
    )(x_cm, sca_cm, sh, scf_cm, *w_args)


def _expert_ffn_body(ntp, slr, ne,
                     te_ref, nu_ref, r0_ref, valid_ref, ilo_ref, ihi_ref, base_ref, cp_ref, lo_ref, used_ref,
                     xsp_ref, xss_ref, wg_ref, bg_ref, wu_ref, bu_ref, wd_ref, bd_ref,
                     ysp_ref, yss_ref,
                     lhs, obuf, zbuf, wg_b, wu_b, wd_b, gsem, ssem, zsem):
    j = pl.program_id(0)
    nu = nu_ref[0]
    nt_all = used_ref.shape[0]

    def for_pieces(step, fn):
        e = te_ref[step]
        r0 = r0_ref[step]
        r1 = r0 + valid_ref[step]

        def body(i, c):
            s0 = base_ref[i * ne + e]
            a = jnp.maximum(s0, r0)
            b = jnp.minimum(s0 + cp_ref[i * ne + e], r1)

            @pl.when(b > a)
            def _():
                n = pl.multiple_of(b - a, SEG_ALIGN)
                loc = lo_ref[i * ne + e] + (a - s0)
                off = pl.multiple_of(a - r0, SEG_ALIGN)

                @pl.when(i < ntp)
                def _():
                    fn(0, pl.multiple_of(i * slr + loc, SEG_ALIGN), off, n)

                @pl.when(i >= ntp)
                def _():
                    fn(1, pl.multiple_of((i - ntp) * slr + loc, SEG_ALIGN), off, n)

            return c

        lax.fori_loop(ilo_ref[step], ihi_ref[step], body, 0)

    def gather_start(step, slot):
        def fn(group, row, off, n):
            src = (xsp_ref, xss_ref)[group]
            pltpu.make_async_copy(src.at[pl.ds(row, n), :], lhs.at[slot, pl.ds(off, n), :], gsem.at[slot]).start()
        for_pieces(step, fn)

    def scatter_start(step, slot):
        def fn(group, row, off, n):
            dst = (ysp_ref, yss_ref)[group]
            pltpu.make_async_copy(obuf.at[slot, pl.ds(off, n), :], dst.at[pl.ds(row, n), :], ssem.at[slot]).start()
        for_pieces(step, fn)

    def gather_wait(step, slot):
        n = pl.multiple_of(valid_ref[step], SEG_ALIGN)
        pltpu.make_async_copy(xsp_ref.at[pl.ds(0, n), :], lhs.at[slot, pl.ds(0, n), :], gsem.at[slot]).wait()

    def scatter_wait(step, slot):
        n = pl.multiple_of(valid_ref[step], SEG_ALIGN)
        pltpu.make_async_copy(obuf.at[slot, pl.ds(0, n), :], ysp_ref.at[pl.ds(0, n), :], ssem.at[slot]).wait()

    def zero_tail(i, do_wait):
        used = pl.multiple_of(used_ref[i], SEG_ALIGN)
        n = pl.multiple_of(slr - used, SEG_ALIGN)

        def go(dst, row):
            cp = pltpu.make_async_copy(zbuf.at[pl.ds(0, n), :], dst.at[pl.ds(row, n), :], zsem)
            cp.wait() if do_wait else cp.start()

        @pl.when((n > 0) & (i < ntp))
        def _():
            go(ysp_ref, pl.multiple_of(i * slr + used, SEG_ALIGN))

        @pl.when((n > 0) & (i >= ntp))
        def _():
            go(yss_ref, pl.multiple_of((i - ntp) * slr + used, SEG_ALIGN))

    @pl.when(j == 0)
    def _():
        lhs[...] = jnp.zeros_like(lhs)
        zbuf[...] = jnp.zeros_like(zbuf)
        lax.fori_loop(0, nt_all, lambda i, c: (zero_tail(i, False), c)[1], 0)
        lax.fori_loop(0, nt_all, lambda i, c: (zero_tail(i, True), c)[1], 0)
        gather_start(0, 0)

    @pl.when(j + 1 < nu)
    def _():
        gather_start(j + 1, (j + 1) % 2)

    slot = j % 2

    @pl.when((j >= 2) & (j - 2 < nu))
    def _():
        scatter_wait(j - 2, slot)

    @pl.when(j < nu)
    def _():
        gather_wait(j, slot)
        prev = te_ref[jnp.maximum(j - 1, 0)]

        @pl.when((j == 0) | (te_ref[j] != prev))
        def _():
            wg_b[...] = wg_ref[...].astype(_BF)
            wu_b[...] = wu_ref[...].astype(_BF)
            wd_b[...] = wd_ref[...].astype(_BF)

        x = lhs[slot].astype(_BF)
        g = jnp.minimum(_dot(x, wg_b[...]) + bg_ref[...], SWIGLU_LIMIT)
        u = jnp.clip(_dot(x, wu_b[...]) + bu_ref[...], -SWIGLU_LIMIT, SWIGLU_LIMIT)
        hdn = (u + 1.0) * (g * _sigmoid(SWIGLU_ALPHA * g))
        obuf[slot] = _dot(hdn.astype(_BF), wd_b[...]) + bd_ref[...]
        scatter_start(j, slot)


def _expert_ffn(meta, xs_p, xs_s, wg, bg, wu, bu, wd, bd, tm, slr, n_tiles):
    ne, d, de = wg.shape
    ntp = xs_p.shape[0] // slr
    n_meta = len(meta)
    wspec = lambda a, b_: pl.BlockSpec((None, a, b_), lambda j, te, *_: (te[j], 0, 0))
    hbm = pl.BlockSpec(memory_space=pl.ANY)
    tail_rows = ne * SEG_ALIGN
    return pl.pallas_call(
        functools.partial(_expert_ffn_body, ntp, slr, ne),
        grid_spec=pltpu.PrefetchScalarGridSpec(
            num_scalar_prefetch=n_meta,
            grid=(n_tiles,),
            in_specs=[hbm, hbm, wspec(d, de), wspec(1, de), wspec(d, de), wspec(1, de), wspec(de, d), wspec(1, d)],
            out_specs=[hbm, hbm],
            scratch_shapes=[
                pltpu.VMEM((2, tm, d), _F32), pltpu.VMEM((2, tm, d), _F32),
                pltpu.VMEM((tail_rows, d), _F32),
                pltpu.VMEM((d, de), _BF), pltpu.VMEM((d, de), _BF), pltpu.VMEM((de, d), _BF),
                pltpu.SemaphoreType.DMA((2,)), pltpu.SemaphoreType.DMA((2,)), pltpu.SemaphoreType.DMA(()),
            ],
        ),
        out_shape=[jax.ShapeDtypeStruct(xs_p.shape, _F32), jax.ShapeDtypeStruct(xs_s.shape, _F32)],
        compiler_params=pltpu.CompilerParams(dimension_semantics=("arbitrary",),
                                             vmem_limit_bytes=VMEM_LIMIT),
        name="expert_ffn",
    )(*meta, xs_p, xs_s, wg, bg, wu, bu, wd, bd)


def _combine_body(ys_ref, h_ref, pos_ref, prob_ref, gfin_ref, y_ref):
    tt, d = h_ref.shape
    slr = ys_ref.shape[0]
    ciota = lax.broadcasted_iota(jnp.int32, (tt, slr), 1).astype(_F32)
    w = jnp.zeros((tt, slr), _F32)
    for k in range(TOP_K):
        w = w + jnp.where(ciota == pos_ref[:, k:k + 1], prob_ref[:, k:k + 1], 0.0)
    out = h_ref[...] + _dot(w.astype(_BF), ys_ref[...].astype(_BF))
    y_ref[...] = _rms(out, gfin_ref[...])


def _combine(ys, h, pos_t, prob_t, g_final, tt, slr):
    n, d = h.shape
    return pl.pallas_call(
        _combine_body,
        grid=(n // tt,),
        in_specs=[pl.BlockSpec((slr, d), lambda i: (i, 0)),
                  pl.BlockSpec((tt, d), lambda i: (i, 0)),
                  pl.BlockSpec((tt, TOP_K), lambda i: (i, 0)),
                  pl.BlockSpec((tt, TOP_K), lambda i: (i, 0)),
                  pl.BlockSpec((1, d), lambda i: (0, 0))],
        out_specs=pl.BlockSpec((tt, d), lambda i: (i, 0)),
        out_shape=jax.ShapeDtypeStruct((n, d), _F32),
        compiler_params=pltpu.CompilerParams(dimension_semantics=("arbitrary",),
                                             vmem_limit_bytes=VMEM_LIMIT),
        name="combine",
    )(ys, h, pos_t, prob_t, g_final)


def _tile_sizes(seq, nb, steps):
    tt = min(256, seq)
    return tt, tt // steps, 512


def _row_tile_tables(counts, tm, n_tiles):
    nt_all, ne = counts.shape
    cp = (counts + (SEG_ALIGN - 1)) // SEG_ALIGN * SEG_ALIGN
    lo = jnp.cumsum(cp, axis=1) - cp
    base = jnp.cumsum(cp, axis=0) - cp
    used = jnp.sum(cp, axis=1)
    total = jnp.sum(cp, axis=0)
    tiles_e = (total + tm - 1) // tm
    tile_end = jnp.cumsum(tiles_e)
    first_tile = tile_end - tiles_e
    ids = jnp.arange(n_tiles, dtype=jnp.int32)
    te = jnp.minimum(jnp.sum((tile_end[None, :] <= ids[:, None]).astype(jnp.int32), axis=1), ne - 1)
    r0 = (ids - first_tile[te]) * tm
    valid = jnp.clip(total[te] - r0, 0, tm)
    seg_start = base.T[te]
    seg_end = seg_start + cp.T[te]
    ilo = jnp.sum((seg_end <= r0[:, None]).astype(jnp.int32), axis=1)
    ihi = jnp.sum((seg_start < (r0 + valid)[:, None]).astype(jnp.int32), axis=1)
    i32 = lambda a: a.astype(jnp.int32)
    return [i32(te), i32(tile_end[-1:]), i32(r0), i32(valid), i32(ilo), i32(ihi),
            i32(base.reshape(-1)), i32(cp.reshape(-1)), i32(lo.reshape(-1)), i32(used)]


def kernel(x_prompt, x_sample, state_conv_a, state_h, state_conf, g_mix, w_in, w_conv_a, b_conv_a,
           w_rg_a, b_rg_a, w_rg_x, b_rg_x, rg_lambda, w_conf_dw, b_conf_dw, g_conf_ln, b_conf_ln,
           w_conf_pw, b_conf_pw, w_out, g_ffn, w_router, b_router, w_gate, b_gate, w_up, b_up,
           w_down, b_down, g_final):
    depth = g_mix.shape[0]
    assert depth == 1
    bsz, seq, d = x_prompt.shape
    nb, steps, _ = x_sample.shape
    ne = w_router.shape[-1]
    kw = w_conv_a.shape[1]
    kc = w_conf_dw.shape[1]
    n_p = bsz * seq
    n_s = nb * steps
    tt, cb, tm = _tile_sizes(seq, nb, steps)
    assert seq % tt == 0 and nb % cb == 0 and cb * steps == tt
    slr = _sorted_rows(tt, ne)
    ntp, nts = n_p // tt, n_s // tt

    row = lambda a: a[0].reshape(1, -1)
    wts = dict(
        gmix=row(g_mix), win=w_in[0].astype(_BF), wca=w_conv_a[0], bca=row(b_conv_a),
        wra=w_rg_a[0].astype(_BF), bra=row(b_rg_a), wrx=w_rg_x[0].astype(_BF), brx=row(b_rg_x),
        lam=row(rg_lambda), wdw=w_conf_dw[0], bdw=row(b_conf_dw), gln=row(g_conf_ln), bln=row(b_conf_ln),
        wpw=w_conf_pw[0].astype(_BF), bpw=row(b_conf_pw), wout=w_out[0].astype(_BF), gffn=row(g_ffn),
        wrt=w_router[0].T.astype(_BF), br=b_router[0].reshape(ne, 1),
    )

    (h_p, xs_p, pos_p, prob_p, cnt_p, pca, ph, pcf) = _prompt_mixer(x_prompt, wts, tt)
    to_cm = lambda a: jnp.swapaxes(a.reshape(nb // cb, cb, a.shape[1], d), 1, 2).reshape(-1, d)
    (h_s, xs_s, pos_s, prob_s, cnt_s, v_s, sh_new, xa_s) = _sample_mixer(
        to_cm(x_sample), to_cm(state_conv_a[0]), state_h[0], to_cm(state_conf[0]), wts, cb, steps)

    counts = jnp.concatenate([cnt_p[:, 0].reshape(ntp, ne), cnt_s[:, 0].reshape(nts, ne)], axis=0)
    n_tiles = ((ntp + nts) * slr) // tm + ne + 2
    meta = _row_tile_tables(counts.astype(jnp.int32), tm, n_tiles)

    ys_p, ys_s = _expert_ffn(meta, xs_p, xs_s, w_gate[0], b_gate[0].reshape(ne, 1, -1), w_up[0],
                             b_up[0].reshape(ne, 1, -1), w_down[0], b_down[0].reshape(ne, 1, -1), tm, slr, n_tiles)
    gfin = g_final.reshape(1, d)
    y_p = _combine(ys_p, h_p.reshape(n_p, d), pos_p.T, prob_p.T, gfin, tt, slr)
    y_s = _combine(ys_s, h_s, pos_s.T, prob_s.T, gfin, tt, slr)

    nat = lambda a: jnp.swapaxes(a.reshape(nb // cb, steps, cb, d), 1, 2).reshape(nb, steps, d)
    y_prompt = y_p.reshape(bsz, seq, d)
    y_sample = nat(y_s)
    p_conv_a = pca[:, CONV_A_HALO - (kw - 1):, :][None]
    p_h = ph.reshape(1, bsz, d)
    p_conf = pcf[:, CONF_HALO - (kc - 1):, :][None]
    s_conv_a = jnp.concatenate([state_conv_a[0], nat(xa_s)], axis=1)[:, -(kw - 1):, :][None]
    s_h = sh_new[None]
    s_conf = jnp.concatenate([state_conf[0], nat(v_s)], axis=1)[:, -(kc - 1):, :][None]
    return (y_prompt, y_sample, p_conv_a, p_h, p_conf, s_conv_a, s_h, s_conf)
```

```python
import functools

import jax
import jax.numpy as jnp
from jax import lax
from jax.experimental import pallas as pl
from jax.experimental.pallas import tpu as pltpu

EPS = 1e-6
RG_C = 8.0
SWIGLU_LIMIT = 7.0
SWIGLU_ALPHA = 1.702
TOP_K = 4
LANE = 128
SUBLANE = 8
CONV_A_HALO = 8
CONF_HALO = 32
TAP_GROUP = 8
SEG_ALIGN = SUBLANE
VMEM_LIMIT = 56 * 1024 * 1024

_BF = jnp.bfloat16
_F32 = jnp.float32


def _const_spec(shape):
    nd = len(shape)
    return pl.BlockSpec(shape, lambda *_: (0,) * nd, pipeline_mode=pl.Buffered(1))


def _sorted_rows(tt, ne):
    return TOP_K * tt + ne * SEG_ALIGN


def _rms(x, g):
    return x * lax.rsqrt(jnp.mean(x * x, axis=-1, keepdims=True) + EPS) * g


def _sigmoid(x):
    return jax.nn.sigmoid(x)


def _dot(a, b):
    return jnp.dot(a, b, preferred_element_type=_F32)


def _block_diag(xb, w_ref, b_ref, out_ref):
    nb, gb, _ = w_ref.shape
    for n in range(nb):
        cs = slice(n * gb, (n + 1) * gb)
        out_ref[:, cs] = _dot(xb[:, cs], w_ref[n]) + b_ref[:, cs]


def _rglru_coeffs(xc_ref, r_ref, i_ref, lam_ref, a_ref, u_ref):
    sp = jax.nn.softplus(-lam_ref[...])
    r = _sigmoid(r_ref[...])
    i = _sigmoid(i_ref[...])
    log_a = -RG_C * r * sp
    a = jnp.exp(log_a)
    a_ref[...] = a
    u_ref[...] = jnp.sqrt(-jnp.tanh(log_a) * (a * a + 1.0)) * (i * xc_ref[...])


def _layernorm_silu(vc, g, b):
    mu = jnp.mean(vc, axis=-1, keepdims=True)
    xc = vc - mu
    y = xc * lax.rsqrt(jnp.mean(xc * xc, axis=-1, keepdims=True) + EPS) * g + b
    return y * _sigmoid(y)


def _route_and_sort(xn_b, wrt_ref, br_ref, pos_ref, prob_ref, cnt_ref, xs_ref):
    ne = wrt_ref.shape[0]
    tt = xn_b.shape[0]
    slr = xs_ref.shape[0]
    logits = lax.dot_general(wrt_ref[...], xn_b, (((1,), (1,)), ((), ())),
                             preferred_element_type=_F32) + br_ref[...]
    eidx = lax.broadcasted_iota(jnp.int32, (ne, tt), 0).astype(_F32)
    work = logits
    vals, onehots = [], []
    for _ in range(TOP_K):
        m = jnp.max(work, axis=0, keepdims=True)
        pick = jnp.min(jnp.where(work == m, eidx, float(ne)), axis=0, keepdims=True)
        one = eidx == pick
        work = jnp.where(one, -jnp.inf, work)
        vals.append(m)
        onehots.append(one)
    exps = [jnp.exp(v - vals[0]) for v in vals]
    inv = 1.0 / (exps[0] + exps[1] + exps[2] + exps[3])
    sel = jnp.zeros((ne, tt), _F32)
    for one in onehots:
        sel = sel + one.astype(_F32)
    row = lax.broadcasted_iota(jnp.int32, (tt, tt), 0)
    col = lax.broadcasted_iota(jnp.int32, (tt, tt), 1)
    incl = _dot(sel.astype(_BF), (row <= col).astype(_BF))
    count = incl[:, tt - 1:tt]
    groups = jnp.floor((count + (SEG_ALIGN - 1)) * (1.0 / SEG_ALIGN))
    er = lax.broadcasted_iota(jnp.int32, (ne, ne), 0)
    ec = lax.broadcasted_iota(jnp.int32, (ne, ne), 1)
    lo = _dot((ec < er).astype(_BF), jnp.broadcast_to(groups, (ne, LANE)).astype(_BF))[:, 0:1] * SEG_ALIGN
    before = lo + (incl - sel)
    riota = lax.broadcasted_iota(jnp.int32, (slr, tt), 0).astype(_F32)
    perm = jnp.zeros((slr, tt), _F32)
    for k in range(TOP_K):
        pos = jnp.sum(jnp.where(onehots[k], before, 0.0), axis=0, keepdims=True)
        pos_ref[k:k + 1, :] = pos
        prob_ref[k:k + 1, :] = exps[k] * inv
        perm = perm + jnp.where(riota == pos, 1.0, 0.0)
    cnt_ref[...] = jnp.broadcast_to(count, cnt_ref.shape)
    xs_ref[...] = _dot(perm.astype(_BF), xn_b)


def _tail(x, ya, yb, ga, gb, wout_ref, gffn_ref, h_ref):
    m = _sigmoid(ga) * ya + _sigmoid(gb) * yb
    h = x + _dot(m.astype(_BF), wout_ref[...])
    h_ref[...] = h
    return _rms(h, gffn_ref[...]).astype(_BF)


_W_NAMES = ["gmix", "win", "wca", "bca", "wra", "bra", "wrx", "brx", "lam", "wdw", "bdw", "gln", "bln",
            "wpw", "bpw", "wout", "gffn", "wrt", "br"]


def _causal_conv_tile(x, ext_ref, w_ref, b_ref, out_ref):
    tt, d = x.shape
    halo = ext_ref.shape[1] - tt
    kk = w_ref.shape[0]
    off = halo - (kk - 1)
    for c in range(d // LANE):
        cs = slice(c * LANE, (c + 1) * LANE)
        ext_ref[c, halo:halo + tt, :] = x[:, cs]
        bb = jnp.broadcast_to(b_ref[0:1, cs], (SUBLANE, LANE))
        for k0 in range(0, kk, TAP_GROUP):
            taps = range(k0, min(k0 + TAP_GROUP, kk))
            wb = {k: jnp.broadcast_to(w_ref[k:k + 1, cs], (SUBLANE, LANE)) for k in taps}
            for r0 in range(0, tt, SUBLANE):
                acc = bb if k0 == 0 else out_ref[r0:r0 + SUBLANE, cs]
                for k in taps:
                    acc = acc + wb[k] * ext_ref[c, r0 + off + k:r0 + off + k + SUBLANE, :]
                out_ref[r0:r0 + SUBLANE, cs] = acc
        ext_ref[c, 0:halo, :] = ext_ref[c, tt:tt + halo, :]


def _prompt_mixer_body(x_ref, gmix_ref, win_ref, wca_ref, bca_ref, wra_ref, bra_ref, wrx_ref, brx_ref,
                       lam_ref, wdw_ref, bdw_ref, gln_ref, bln_ref, wpw_ref, bpw_ref, wout_ref,
                       gffn_ref, wrt_ref, br_ref,
                       h_ref, xs_ref, pos_ref, prob_ref, cnt_ref, pca_ref, ph_ref, pcf_ref,
                       xa_ext, v_ext, s0, s1, s2, s3, s4, hcar):
    tt, d = x_ref.shape
    j = pl.program_id(1)

    @pl.when(j == 0)
    def _():
        hcar[...] = jnp.zeros_like(hcar)
        xa_ext[:, 0:CONV_A_HALO, :] = jnp.zeros((d // LANE, CONV_A_HALO, LANE), _F32)
        v_ext[:, 0:CONF_HALO, :] = jnp.zeros((d // LANE, CONF_HALO, LANE), _F32)

    x = x_ref[...]
    hn = _rms(x, gmix_ref[...]).astype(_BF)

    xa = _dot(hn, win_ref[:, 0:d])
    _causal_conv_tile(xa, xa_ext, wca_ref, bca_ref, s4)
    pca_ref[...] = xa[tt - CONV_A_HALO:tt, :]
    xcb = s4[...].astype(_BF)
    _block_diag(xcb, wra_ref, bra_ref, s0)
    _block_diag(xcb, wrx_ref, brx_ref, s1)
    _rglru_coeffs(s4, s0, s1, lam_ref, s2, s3)

    ri = lax.broadcasted_iota(jnp.int32, (SUBLANE, d), 0)

    h_last = hcar[...]
    for r0 in range(0, tt, SUBLANE):
        a = s2[r0:r0 + SUBLANE, :]
        u = s3[r0:r0 + SUBLANE, :]
        for s in (1, 2, 4):
            keep = ri >= s
            a_sh = jnp.where(keep, pltpu.roll(a, s, axis=0), 1.0)
            u_sh = jnp.where(keep, pltpu.roll(u, s, axis=0), 0.0)
            u = a * u_sh + u
            a = a * a_sh
        hh = a * h_last + u
        s0[r0:r0 + SUBLANE, :] = hh
        h_last = jnp.broadcast_to(hh[SUBLANE - 1:SUBLANE, :], (SUBLANE, d))
    hcar[...] = h_last
    ph_ref[...] = h_last[0:1, :]

    v = _dot(hn, win_ref[:, d:2 * d]) * _sigmoid(_dot(hn, win_ref[:, 2 * d:3 * d]))
    pcf_ref[...] = v[tt - CONF_HALO:tt, :]
    _causal_conv_tile(v, v_ext, wdw_ref, bdw_ref, s1)
    sl = _layernorm_silu(s1[...], gln_ref[...], bln_ref[...])
    yb = _dot(sl.astype(_BF), wpw_ref[...]) + bpw_ref[...]

    ga = _dot(hn, win_ref[:, 3 * d:4 * d])
    gb = _dot(hn, win_ref[:, 4 * d:5 * d])
    xn_b = _tail(x, s0[...], yb, ga, gb, wout_ref, gffn_ref, h_ref)
    _route_and_sort(xn_b, wrt_ref, br_ref, pos_ref, prob_ref, cnt_ref, xs_ref)


def _routing_outputs(n_tiles, tt, d, ne):
    slr = _sorted_rows(tt, ne)
    shapes = [
        jax.ShapeDtypeStruct((n_tiles * slr, d), _F32),
        jax.ShapeDtypeStruct((TOP_K, n_tiles * tt), _F32),
        jax.ShapeDtypeStruct((TOP_K, n_tiles * tt), _F32),
        jax.ShapeDtypeStruct((n_tiles * ne, LANE), _F32),
    ]
    blocks = [(slr, d), (TOP_K, tt), (TOP_K, tt), (ne, LANE)]
    return shapes, blocks


def _prompt_mixer(x, wts, tt):
    bsz, t, d = x.shape
    nt = t // tt
    ne = wts["wrt"].shape[0]
    w_args = [wts[n] for n in _W_NAMES]
    r_shapes, r_blocks = _routing_outputs(bsz * nt, tt, d, ne)
    tile_major = lambda blk: pl.BlockSpec(blk, (lambda b, j: (b * nt + j, 0)) if blk[0] != TOP_K
                                          else (lambda b, j: (0, b * nt + j)))
    out_shape = [jax.ShapeDtypeStruct((bsz, t, d), _F32)] + r_shapes + [
        jax.ShapeDtypeStruct((bsz, CONV_A_HALO, d), _F32),
        jax.ShapeDtypeStruct((bsz, 1, d), _F32),
        jax.ShapeDtypeStruct((bsz, CONF_HALO, d), _F32),
    ]
    out_specs = [pl.BlockSpec((None, tt, d), lambda b, j: (b, j, 0))] + [tile_major(blk) for blk in r_blocks] + [
        pl.BlockSpec((None, CONV_A_HALO, d), lambda b, j: (b, 0, 0)),
        pl.BlockSpec((None, 1, d), lambda b, j: (b, 0, 0)),
        pl.BlockSpec((None, CONF_HALO, d), lambda b, j: (b, 0, 0)),
    ]
    scratch = [
        pltpu.VMEM((d // LANE, tt + CONV_A_HALO, LANE), _F32),
        pltpu.VMEM((d // LANE, tt + CONF_HALO, LANE), _F32),
        pltpu.VMEM((tt, d), _F32), pltpu.VMEM((tt, d), _F32), pltpu.VMEM((tt, d), _F32),
        pltpu.VMEM((tt, d), _F32), pltpu.VMEM((tt, d), _F32),
        pltpu.VMEM((SUBLANE, d), _F32),
    ]
    return pl.pallas_call(
        _prompt_mixer_body,
        grid=(bsz, nt),
        in_specs=[pl.BlockSpec((None, tt, d), lambda b, j: (b, j, 0))] + [_const_spec(w.shape) for w in w_args],
        out_specs=out_specs,
        out_shape=out_shape,
        scratch_shapes=scratch,
        compiler_params=pltpu.CompilerParams(dimension_semantics=("arbitrary", "arbitrary"),
                                             vmem_limit_bytes=VMEM_LIMIT),
        name="prompt_mixer",
    )(x, *w_args)


def _sample_mixer_body(nb, x_ref, sca_ref, sh_ref, scf_ref,
                       gmix_ref, win_ref, wca_ref, bca_ref, wra_ref, bra_ref, wrx_ref, brx_ref,
                       lam_ref, wdw_ref, bdw_ref, gln_ref, bln_ref, wpw_ref, bpw_ref, wout_ref,
                       gffn_ref, wrt_ref, br_ref,
                       h_ref, xs_ref, pos_ref, prob_ref, cnt_ref, v_out_ref, sh_out_ref, xa_out_ref,
                       s0, s1, s2, s3, s4):
    n, d = x_ref.shape
    steps = n // nb
    x = x_ref[...]
    hn = _rms(x, gmix_ref[...]).astype(_BF)

    xa = _dot(hn, win_ref[:, 0:d])
    xa_out_ref[...] = xa
    kw = wca_ref.shape[0]

    def xcat_a(slab):
        if slab < kw - 1:
            return sca_ref[slab * nb:(slab + 1) * nb, :]
        s = slab - (kw - 1)
        return xa[s * nb:(s + 1) * nb, :]

    for t in range(steps):
        acc = bca_ref[...] + jnp.zeros((nb, d), _F32)
        for k in range(kw):
            acc = acc + wca_ref[k:k + 1, :] * xcat_a(t + k)
        s4[t * nb:(t + 1) * nb, :] = acc
    xcb = s4[...].astype(_BF)
    _block_diag(xcb, wra_ref, bra_ref, s0)
    _block_diag(xcb, wrx_ref, brx_ref, s1)
    _rglru_coeffs(s4, s0, s1, lam_ref, s2, s3)
    h = sh_ref[...]
    for t in range(steps):
        h = s2[t * nb:(t + 1) * nb, :] * h + s3[t * nb:(t + 1) * nb, :]
        s0[t * nb:(t + 1) * nb, :] = h
    sh_out_ref[...] = h

    v = _dot(hn, win_ref[:, d:2 * d]) * _sigmoid(_dot(hn, win_ref[:, 2 * d:3 * d]))
    v_out_ref[...] = v
    kc = wdw_ref.shape[0]

    def xcat_v(slab):
        if slab < kc - 1:
            return scf_ref[slab * nb:(slab + 1) * nb, :]
        s = slab - (kc - 1)
        return v[s * nb:(s + 1) * nb, :]

    for t in range(steps):
        acc = bdw_ref[...] + jnp.zeros((nb, d), _F32)
        for k in range(kc):
            acc = acc + wdw_ref[k:k + 1, :] * xcat_v(t + k)
        s1[t * nb:(t + 1) * nb, :] = acc
    sl = _layernorm_silu(s1[...], gln_ref[...], bln_ref[...])
    yb = _dot(sl.astype(_BF), wpw_ref[...]) + bpw_ref[...]

    ga = _dot(hn, win_ref[:, 3 * d:4 * d])
    gb = _dot(hn, win_ref[:, 4 * d:5 * d])
    xn_b = _tail(x, s0[...], yb, ga, gb, wout_ref, gffn_ref, h_ref)
    _route_and_sort(xn_b, wrt_ref, br_ref, pos_ref, prob_ref, cnt_ref, xs_ref)


def _sample_mixer(x_cm, sca_cm, sh, scf_cm, wts, cb, steps):
    n, d = x_cm.shape
    rows = cb * steps
    n_chunks = n // rows
    ne = wts["wrt"].shape[0]
    kw = wts["wca"].shape[0]
    kc = wts["wdw"].shape[0]
    w_args = [wts[n_] for n_ in _W_NAMES]
    chunk = lambda r: pl.BlockSpec((r, d), lambda i: (i, 0))
    r_shapes, r_blocks = _routing_outputs(n_chunks, rows, d, ne)
    tile_major = lambda blk: pl.BlockSpec(blk, (lambda i: (i, 0)) if blk[0] != TOP_K else (lambda i: (0, i)))
    conf_state = pl.BlockSpec(((kc - 1) * cb, d), lambda i: (i, 0), pipeline_mode=pl.Buffered(1))
    in_specs = [chunk(rows), chunk((kw - 1) * cb), chunk(cb), conf_state] + \
        [_const_spec(w.shape) for w in w_args]
    out_shape = [jax.ShapeDtypeStruct((n, d), _F32)] + r_shapes + [
        jax.ShapeDtypeStruct((n, d), _F32),
        jax.ShapeDtypeStruct(sh.shape, _F32),
        jax.ShapeDtypeStruct((n, d), _F32),
    ]
    out_specs = [chunk(rows)] + [tile_major(blk) for blk in r_blocks] + [chunk(rows), chunk(cb), chunk(rows)]
    scratch = [pltpu.VMEM((rows, d), _F32)] * 5
    return pl.pallas_call(
        functools.partial(_sample_mixer_body, cb),
        grid=(n_chunks,),
        in_specs=in_specs,
        out_specs=out_specs,
        out_shape=out_shape,
        scratch_shapes=scratch,
        compiler_params=pltpu.CompilerParams(dimension_semantics=("arbitrary",),
                                             vmem_limit_bytes=VMEM_LIMIT),
        name="sample_mixer",
    )(x_cm, sca_cm, sh, scf_cm, *w_args)


def _expert_ffn_body(ntp,
                     te_ref, nu_ref, valid_ref, ilo_ref, ihi_ref, psrc_ref, poff_ref, pn_ref, tail_ref,
                     xsp_ref, xss_ref, wg_ref, bg_ref, wu_ref, bu_ref, wd_ref, bd_ref,
                     ysp_ref, yss_ref,
                     lhs, obuf, zbuf, wg_b, wu_b, wd_b, gsem, ssem, zsem):
    j = pl.program_id(0)
    nu = nu_ref[0]
    nt_all = tail_ref.shape[0] // 2
    tm = lhs.shape[1]

    def for_pieces(step, fn):
        row0 = step * nt_all

        def run(group, first, last):
            def body(i, c):
                n = pn_ref[row0 + i]

                @pl.when(n > 0)
                def _():
                    fn(group, pl.multiple_of(psrc_ref[row0 + i], SEG_ALIGN),
                       pl.multiple_of(poff_ref[row0 + i], SEG_ALIGN), pl.multiple_of(n, SEG_ALIGN))

                return c

            lax.fori_loop(first, last, body, 0)

        run(0, ilo_ref[step], jnp.minimum(ihi_ref[step], ntp))
        run(1, jnp.maximum(ilo_ref[step], ntp), ihi_ref[step])

    def gather_start(step, slot):
        def fn(group, row, off, n):
            src = (xsp_ref, xss_ref)[group]
            pltpu.make_async_copy(src.at[pl.ds(row, n), :], lhs.at[slot, pl.ds(off, n), :], gsem.at[slot]).start()
        for_pieces(step, fn)

    def scatter_start(step, slot):
        def fn(group, row, off, n):
            dst = (ysp_ref, yss_ref)[group]
            pltpu.make_async_copy(obuf.at[slot, pl.ds(off, n), :], dst.at[pl.ds(row, n), :], ssem.at[slot]).start()
        for_pieces(step, fn)

    def gather_wait(step, slot):
        n = pl.multiple_of(valid_ref[step], SEG_ALIGN)
        pltpu.make_async_copy(xsp_ref.at[pl.ds(0, n), :], lhs.at[slot, pl.ds(0, n), :], gsem.at[slot]).wait()

    def scatter_wait(step, slot):
        n = pl.multiple_of(valid_ref[step], SEG_ALIGN)
        pltpu.make_async_copy(obuf.at[slot, pl.ds(0, n), :], ysp_ref.at[pl.ds(0, n), :], ssem.at[slot]).wait()

    def zero_tails(do_wait):
        def run(group, first, last):
            def one(i, c):
                row = pl.multiple_of(tail_ref[2 * i], SEG_ALIGN)
                n = pl.multiple_of(tail_ref[2 * i + 1], SEG_ALIGN)

                @pl.when(n > 0)
                def _():
                    dst = (ysp_ref, yss_ref)[group]
                    cp = pltpu.make_async_copy(zbuf.at[pl.ds(0, n), :], dst.at[pl.ds(row, n), :], zsem)
                    cp.wait() if do_wait else cp.start()

                return c

            lax.fori_loop(first, last, one, 0)

        run(0, 0, ntp)
        run(1, ntp, nt_all)

    @pl.when(j == 0)
    def _():
        lhs[...] = jnp.zeros_like(lhs)
        zbuf[...] = jnp.zeros_like(zbuf)
        zero_tails(False)
        zero_tails(True)
        gather_start(0, 0)

    @pl.when(j + 1 < nu)
    def _():
        gather_start(j + 1, (j + 1) % 2)

    slot = j % 2

    @pl.when((j >= 2) & (j - 2 < nu))
    def _():
        scatter_wait(j - 2, slot)

    def compute(rows):
        x = lhs[slot, 0:rows, :].astype(_BF)
        g = jnp.minimum(_dot(x, wg_b[...]) + bg_ref[...], SWIGLU_LIMIT)
        u = jnp.clip(_dot(x, wu_b[...]) + bu_ref[...], -SWIGLU_LIMIT, SWIGLU_LIMIT)
        hdn = (u + 1.0) * (g * _sigmoid(SWIGLU_ALPHA * g))
        obuf[slot, 0:rows, :] = _dot(hdn.astype(_BF), wd_b[...]) + bd_ref[...]

    @pl.when(j < nu)
    def _():
        gather_wait(j, slot)
        prev = te_ref[jnp.maximum(j - 1, 0)]

        @pl.when((j == 0) | (te_ref[j] != prev))
        def _():
            wg_b[...] = wg_ref[...].astype(_BF)
            wu_b[...] = wu_ref[...].astype(_BF)
            wd_b[...] = wd_ref[...].astype(_BF)

        @pl.when(valid_ref[j] > tm // 2)
        def _():
            compute(tm)

        @pl.when(valid_ref[j] <= tm // 2)
        def _():
            compute(tm // 2)

        scatter_start(j, slot)


def _expert_ffn(meta, xs_p, xs_s, wg, bg, wu, bu, wd, bd, tm, slr, n_tiles):
    ne, d, de = wg.shape
    ntp = xs_p.shape[0] // slr
    n_meta = len(meta)
    wspec = lambda a, b_: pl.BlockSpec((None, a, b_), lambda j, te, *_: (te[j], 0, 0))
    hbm = pl.BlockSpec(memory_space=pl.ANY)
    tail_rows = ne * SEG_ALIGN
    return pl.pallas_call(
        functools.partial(_expert_ffn_body, ntp),
        grid_spec=pltpu.PrefetchScalarGridSpec(
            num_scalar_prefetch=n_meta,
            grid=(n_tiles,),
            in_specs=[hbm, hbm, wspec(d, de), wspec(1, de), wspec(d, de), wspec(1, de), wspec(de, d), wspec(1, d)],
            out_specs=[hbm, hbm],
            scratch_shapes=[
                pltpu.VMEM((2, tm, d), _F32), pltpu.VMEM((2, tm, d), _F32),
                pltpu.VMEM((tail_rows, d), _F32),
                pltpu.VMEM((d, de), _BF), pltpu.VMEM((d, de), _BF), pltpu.VMEM((de, d), _BF),
                pltpu.SemaphoreType.DMA((2,)), pltpu.SemaphoreType.DMA((2,)), pltpu.SemaphoreType.DMA(()),
            ],
        ),
        out_shape=[jax.ShapeDtypeStruct(xs_p.shape, _F32), jax.ShapeDtypeStruct(xs_s.shape, _F32)],
        compiler_params=pltpu.CompilerParams(dimension_semantics=("arbitrary",),
                                             vmem_limit_bytes=VMEM_LIMIT),
        name="expert_ffn",
    )(*meta, xs_p, xs_s, wg, bg, wu, bu, wd, bd)


def _combine_body(ys_ref, h_ref, pos_ref, prob_ref, gfin_ref, y_ref):
    tt, d = h_ref.shape
    slr = ys_ref.shape[0]
    ciota = lax.broadcasted_iota(jnp.int32, (tt, slr), 1).astype(_F32)
    w = jnp.zeros((tt, slr), _F32)
    for k in range(TOP_K):
        w = w + jnp.where(ciota == pos_ref[:, k:k + 1], prob_ref[:, k:k + 1], 0.0)
    out = h_ref[...] + _dot(w.astype(_BF), ys_ref[...].astype(_BF))
    y_ref[...] = _rms(out, gfin_ref[...])


def _combine(ys, h, pos_t, prob_t, g_final, tt, slr):
    n, d = h.shape
    return pl.pallas_call(
        _combine_body,
        grid=(n // tt,),
        in_specs=[pl.BlockSpec((slr, d), lambda i: (i, 0)),
                  pl.BlockSpec((tt, d), lambda i: (i, 0)),
                  pl.BlockSpec((tt, TOP_K), lambda i: (i, 0)),
                  pl.BlockSpec((tt, TOP_K), lambda i: (i, 0)),
                  pl.BlockSpec((1, d), lambda i: (0, 0))],
        out_specs=pl.BlockSpec((tt, d), lambda i: (i, 0)),
        out_shape=jax.ShapeDtypeStruct((n, d), _F32),
        compiler_params=pltpu.CompilerParams(dimension_semantics=("arbitrary",),
                                             vmem_limit_bytes=VMEM_LIMIT),
        name="combine",
    )(ys, h, pos_t, prob_t, g_final)


def _tile_sizes(seq, nb, steps):
    tt = min(256, seq)
    return tt, tt // steps, 512


def _row_tile_tables(counts, tm, n_tiles, ntp, slr):
    nt_all, ne = counts.shape
    cp = (counts + (SEG_ALIGN - 1)) // SEG_ALIGN * SEG_ALIGN
    lo = jnp.cumsum(cp, axis=1) - cp
    base = jnp.cumsum(cp, axis=0) - cp
    used = jnp.sum(cp, axis=1)
    total = jnp.sum(cp, axis=0)
    tiles_e = (total + tm - 1) // tm
    tile_end = jnp.cumsum(tiles_e)
    first_tile = tile_end - tiles_e
    ids = jnp.arange(n_tiles, dtype=jnp.int32)
    te = jnp.minimum(jnp.sum((tile_end[None, :] <= ids[:, None]).astype(jnp.int32), axis=1), ne - 1)
    r0 = (ids - first_tile[te]) * tm
    valid = jnp.clip(total[te] - r0, 0, tm)
    seg_start = base.T[te]
    seg_end = seg_start + cp.T[te]
    r1 = r0 + valid
    ilo = jnp.sum((seg_end <= r0[:, None]).astype(jnp.int32), axis=1)
    ihi = jnp.sum((seg_start < r1[:, None]).astype(jnp.int32), axis=1)
    first = jnp.maximum(seg_start, r0[:, None])
    p_n = jnp.maximum(jnp.minimum(seg_end, r1[:, None]) - first, 0)
    tile_row0 = jnp.where(jnp.arange(nt_all) < ntp, jnp.arange(nt_all), jnp.arange(nt_all) - ntp) * slr
    p_src = tile_row0[None, :] + lo.T[te] + (first - seg_start)
    p_off = first - r0[:, None]
    tails = jnp.stack([tile_row0 + used, slr - used], axis=1)
    i32 = lambda a: a.astype(jnp.int32).reshape(-1)
    return [i32(te), i32(tile_end[-1:]), i32(valid), i32(ilo), i32(ihi), i32(p_src), i32(p_off), i32(p_n),
            i32(tails)]


def kernel(x_prompt, x_sample, state_conv_a, state_h, state_conf, g_mix, w_in, w_conv_a, b_conv_a,
           w_rg_a, b_rg_a, w_rg_x, b_rg_x, rg_lambda, w_conf_dw, b_conf_dw, g_conf_ln, b_conf_ln,
           w_conf_pw, b_conf_pw, w_out, g_ffn, w_router, b_router, w_gate, b_gate, w_up, b_up,
           w_down, b_down, g_final):
    depth = g_mix.shape[0]
    assert depth == 1
    bsz, seq, d = x_prompt.shape
    nb, steps, _ = x_sample.shape
    ne = w_router.shape[-1]
    kw = w_conv_a.shape[1]
    kc = w_conf_dw.shape[1]
    n_p = bsz * seq
    n_s = nb * steps
    tt, cb, tm = _tile_sizes(seq, nb, steps)
    assert seq % tt == 0 and nb % cb == 0 and cb * steps == tt
    slr = _sorted_rows(tt, ne)
    ntp, nts = n_p // tt, n_s // tt

    row = lambda a: a[0].reshape(1, -1)
    wts = dict(
        gmix=row(g_mix), win=w_in[0].astype(_BF), wca=w_conv_a[0], bca=row(b_conv_a),
        wra=w_rg_a[0].astype(_BF), bra=row(b_rg_a), wrx=w_rg_x[0].astype(_BF), brx=row(b_rg_x),
        lam=row(rg_lambda), wdw=w_conf_dw[0], bdw=row(b_conf_dw), gln=row(g_conf_ln), bln=row(b_conf_ln),
        wpw=w_conf_pw[0].astype(_BF), bpw=row(b_conf_pw), wout=w_out[0].astype(_BF), gffn=row(g_ffn),
        wrt=w_router[0].T.astype(_BF), br=b_router[0].reshape(ne, 1),
    )

    (h_p, xs_p, pos_p, prob_p, cnt_p, pca, ph, pcf) = _prompt_mixer(x_prompt, wts, tt)
    to_cm = lambda a: jnp.swapaxes(a.reshape(nb // cb, cb, a.shape[1], d), 1, 2).reshape(-1, d)
    (h_s, xs_s, pos_s, prob_s, cnt_s, v_s, sh_new, xa_s) = _sample_mixer(
        to_cm(x_sample), to_cm(state_conv_a[0]), state_h[0], to_cm(state_conf[0]), wts, cb, steps)

    counts = jnp.concatenate([cnt_p[:, 0].reshape(ntp, ne), cnt_s[:, 0].reshape(nts, ne)], axis=0)
    n_tiles = ((ntp + nts) * slr) // tm + ne + 2
    meta = _row_tile_tables(counts.astype(jnp.int32), tm, n_tiles, ntp, slr)

    ys_p, ys_s = _expert_ffn(meta, xs_p, xs_s, w_gate[0], b_gate[0].reshape(ne, 1, -1), w_up[0],
                             b_up[0].reshape(ne, 1, -1), w_down[0], b_down[0].reshape(ne, 1, -1), tm, slr, n_tiles)
    gfin = g_final.reshape(1, d)
    y_p = _combine(ys_p, h_p.reshape(n_p, d), pos_p.T, prob_p.T, gfin, tt, slr)
    y_s = _combine(ys_s, h_s, pos_s.T, prob_s.T, gfin, tt, slr)

    nat = lambda a: jnp.swapaxes(a.reshape(nb // cb, steps, cb, d), 1, 2).reshape(nb, steps, d)
    y_prompt = y_p.reshape(bsz, seq, d)
    y_sample = nat(y_s)
    p_conv_a = pca[:, CONV_A_HALO - (kw - 1):, :][None]
    p_h = ph.reshape(1, bsz, d)
    p_conf = pcf[:, CONF_HALO - (kc - 1):, :][None]
    s_conv_a = jnp.concatenate([state_conv_a[0], nat(xa_s)], axis=1)[:, -(kw - 1):, :][None]
    s_h = sh_new[None]
    s_conf = jnp.concatenate([state_conf[0], nat(v_s)], axis=1)[:, -(kc - 1):, :][None]
    return (y_prompt, y_sample, p_conv_a, p_h, p_conf, s_conv_a, s_h, s_conf)
```

```python
import functools

import jax
import jax.numpy as jnp
from jax import lax
from jax.experimental import pallas as pl
from jax.experimental.pallas import tpu as pltpu

EPS = 1e-6
RG_C = 8.0
SWIGLU_LIMIT = 7.0
SWIGLU_ALPHA = 1.702
TOP_K = 4
LANE = 128
SUBLANE = 8
CONV_A_HALO = 8
CONF_HALO = 32
TAP_GROUP = 8
SEG_ALIGN = SUBLANE
VMEM_LIMIT = 56 * 1024 * 1024

_BF = jnp.bfloat16
_F32 = jnp.float32


def _const_spec(shape):
    nd = len(shape)
    return pl.BlockSpec(shape, lambda *_: (0,) * nd, pipeline_mode=pl.Buffered(1))


def _sorted_rows(tt, ne):
    return TOP_K * tt + ne * SEG_ALIGN


def _rms(x, g):
    return x * lax.rsqrt(jnp.mean(x * x, axis=-1, keepdims=True) + EPS) * g


def _sigmoid(x):
    return jax.nn.sigmoid(x)


def _dot(a, b):
    return jnp.dot(a, b, preferred_element_type=_F32)


def _block_diag(xb, w_ref, b_ref, out_ref):
    nb, gb, _ = w_ref.shape
    for n in range(nb):
        cs = slice(n * gb, (n + 1) * gb)
        out_ref[:, cs] = _dot(xb[:, cs], w_ref[n]) + b_ref[:, cs]


def _rglru_coeffs(xc_ref, r_ref, i_ref, lam_ref, a_ref, u_ref):
    sp = jax.nn.softplus(-lam_ref[...])
    r = _sigmoid(r_ref[...])
    i = _sigmoid(i_ref[...])
    log_a = -RG_C * r * sp
    a = jnp.exp(log_a)
    a_ref[...] = a
    u_ref[...] = jnp.sqrt(-jnp.tanh(log_a) * (a * a + 1.0)) * (i * xc_ref[...])


def _layernorm_silu(vc, g, b):
    mu = jnp.mean(vc, axis=-1, keepdims=True)
    xc = vc - mu
    y = xc * lax.rsqrt(jnp.mean(xc * xc, axis=-1, keepdims=True) + EPS) * g + b
    return y * _sigmoid(y)


def _route_and_sort(xn_b, wrt_ref, br_ref, pos_ref, prob_ref, cnt_ref, xs_ref):
    ne = wrt_ref.shape[0]
    tt = xn_b.shape[0]
    slr = xs_ref.shape[0]
    logits = lax.dot_general(wrt_ref[...], xn_b, (((1,), (1,)), ((), ())),
                             preferred_element_type=_F32) + br_ref[...]
    eidx = lax.broadcasted_iota(jnp.int32, (ne, tt), 0).astype(_F32)
    work = logits
    vals, onehots = [], []
    for _ in range(TOP_K):
        m = jnp.max(work, axis=0, keepdims=True)
        pick = jnp.min(jnp.where(work == m, eidx, float(ne)), axis=0, keepdims=True)
        one = eidx == pick
        work = jnp.where(one, -jnp.inf, work)
        vals.append(m)
        onehots.append(one)
    exps = [jnp.exp(v - vals[0]) for v in vals]
    inv = 1.0 / (exps[0] + exps[1] + exps[2] + exps[3])
    sel = jnp.zeros((ne, tt), _F32)
    for one in onehots:
        sel = sel + one.astype(_F32)
    row = lax.broadcasted_iota(jnp.int32, (tt, tt), 0)
    col = lax.broadcasted_iota(jnp.int32, (tt, tt), 1)
    incl = _dot(sel.astype(_BF), (row <= col).astype(_BF))
    count = incl[:, tt - 1:tt]
    groups = jnp.floor((count + (SEG_ALIGN - 1)) * (1.0 / SEG_ALIGN))
    er = lax.broadcasted_iota(jnp.int32, (ne, ne), 0)
    ec = lax.broadcasted_iota(jnp.int32, (ne, ne), 1)
    lo = _dot((ec < er).astype(_BF), jnp.broadcast_to(groups, (ne, LANE)).astype(_BF))[:, 0:1] * SEG_ALIGN
    before = lo + (incl - sel)
    riota = lax.broadcasted_iota(jnp.int32, (slr, tt), 0).astype(_F32)
    perm = jnp.zeros((slr, tt), _F32)
    for k in range(TOP_K):
        pos = jnp.sum(jnp.where(onehots[k], before, 0.0), axis=0, keepdims=True)
        pos_ref[k:k + 1, :] = pos
        prob_ref[k:k + 1, :] = exps[k] * inv
        perm = jnp.where(riota == pos, 1.0, perm)
    cnt_ref[...] = jnp.broadcast_to(count, cnt_ref.shape)
    xs_ref[...] = _dot(perm.astype(_BF), xn_b)


def _tail(x, ya, yb, ga, gb, wout_ref, gffn_ref, h_ref):
    m = _sigmoid(ga) * ya + _sigmoid(gb) * yb
    h = x + _dot(m.astype(_BF), wout_ref[...])
    h_ref[...] = h
    return _rms(h, gffn_ref[...]).astype(_BF)


_W_NAMES = ["gmix", "win", "wca", "bca", "wra", "bra", "wrx", "brx", "lam", "wdw", "bdw", "gln", "bln",
            "wpw", "bpw", "wout", "gffn", "wrt", "br"]


def _causal_conv_tile(x, ext_ref, w_ref, b_ref, out_ref):
    tt, d = x.shape
    halo = ext_ref.shape[1] - tt
    kk = w_ref.shape[0]
    off = halo - (kk - 1)
    for c in range(d // LANE):
        cs = slice(c * LANE, (c + 1) * LANE)
        ext_ref[c, halo:halo + tt, :] = x[:, cs]
        bb = jnp.broadcast_to(b_ref[0:1, cs], (SUBLANE, LANE))
        for k0 in range(0, kk, TAP_GROUP):
            taps = range(k0, min(k0 + TAP_GROUP, kk))
            wb = {k: jnp.broadcast_to(w_ref[k:k + 1, cs], (SUBLANE, LANE)) for k in taps}
            for r0 in range(0, tt, SUBLANE):
                acc = bb if k0 == 0 else out_ref[r0:r0 + SUBLANE, cs]
                for k in taps:
                    acc = acc + wb[k] * ext_ref[c, r0 + off + k:r0 + off + k + SUBLANE, :]
                out_ref[r0:r0 + SUBLANE, cs] = acc
        ext_ref[c, 0:halo, :] = ext_ref[c, tt:tt + halo, :]


def _prompt_mixer_body(x_ref, gmix_ref, win_ref, wca_ref, bca_ref, wra_ref, bra_ref, wrx_ref, brx_ref,
                       lam_ref, wdw_ref, bdw_ref, gln_ref, bln_ref, wpw_ref, bpw_ref, wout_ref,
                       gffn_ref, wrt_ref, br_ref,
                       h_ref, xs_ref, pos_ref, prob_ref, cnt_ref, pca_ref, ph_ref, pcf_ref,
                       xa_ext, v_ext, s0, s1, s2, s3, s4, hcar):
    tt, d = x_ref.shape
    j = pl.program_id(1)

    @pl.when(j == 0)
    def _():
        hcar[...] = jnp.zeros_like(hcar)
        xa_ext[:, 0:CONV_A_HALO, :] = jnp.zeros((d // LANE, CONV_A_HALO, LANE), _F32)
        v_ext[:, 0:CONF_HALO, :] = jnp.zeros((d // LANE, CONF_HALO, LANE), _F32)

    x = x_ref[...]
    hn = _rms(x, gmix_ref[...]).astype(_BF)

    xa = _dot(hn, win_ref[:, 0:d])
    _causal_conv_tile(xa, xa_ext, wca_ref, bca_ref, s4)
    pca_ref[...] = xa[tt - CONV_A_HALO:tt, :]
    xcb = s4[...].astype(_BF)
    _block_diag(xcb, wra_ref, bra_ref, s0)
    _block_diag(xcb, wrx_ref, brx_ref, s1)
    _rglru_coeffs(s4, s0, s1, lam_ref, s2, s3)

    ri = lax.broadcasted_iota(jnp.int32, (SUBLANE, d), 0)

    h_last = hcar[...]
    for r0 in range(0, tt, SUBLANE):
        a = s2[r0:r0 + SUBLANE, :]
        u = s3[r0:r0 + SUBLANE, :]
        for s in (1, 2, 4):
            keep = ri >= s
            a_sh = jnp.where(keep, pltpu.roll(a, s, axis=0), 1.0)
            u_sh = jnp.where(keep, pltpu.roll(u, s, axis=0), 0.0)
            u = a * u_sh + u
            a = a * a_sh
        hh = a * h_last + u
        s0[r0:r0 + SUBLANE, :] = hh
        h_last = jnp.broadcast_to(hh[SUBLANE - 1:SUBLANE, :], (SUBLANE, d))
    hcar[...] = h_last
    ph_ref[...] = h_last[0:1, :]

    v = _dot(hn, win_ref[:, d:2 * d]) * _sigmoid(_dot(hn, win_ref[:, 2 * d:3 * d]))
    pcf_ref[...] = v[tt - CONF_HALO:tt, :]
    _causal_conv_tile(v, v_ext, wdw_ref, bdw_ref, s1)
    sl = _layernorm_silu(s1[...], gln_ref[...], bln_ref[...])
    yb = _dot(sl.astype(_BF), wpw_ref[...]) + bpw_ref[...]

    ga = _dot(hn, win_ref[:, 3 * d:4 * d])
    gb = _dot(hn, win_ref[:, 4 * d:5 * d])
    xn_b = _tail(x, s0[...], yb, ga, gb, wout_ref, gffn_ref, h_ref)
    _route_and_sort(xn_b, wrt_ref, br_ref, pos_ref, prob_ref, cnt_ref, xs_ref)


def _routing_outputs(n_tiles, tt, d, ne):
    slr = _sorted_rows(tt, ne)
    shapes = [
        jax.ShapeDtypeStruct((n_tiles * slr, d), _F32),
        jax.ShapeDtypeStruct((TOP_K, n_tiles * tt), _F32),
        jax.ShapeDtypeStruct((TOP_K, n_tiles * tt), _F32),
        jax.ShapeDtypeStruct((n_tiles * ne, LANE), _F32),
    ]
    blocks = [(slr, d), (TOP_K, tt), (TOP_K, tt), (ne, LANE)]
    return shapes, blocks


def _prompt_mixer(x, wts, tt):
    bsz, t, d = x.shape
    nt = t // tt
    ne = wts["wrt"].shape[0]
    w_args = [wts[n] for n in _W_NAMES]
    r_shapes, r_blocks = _routing_outputs(bsz * nt, tt, d, ne)
    tile_major = lambda blk: pl.BlockSpec(blk, (lambda b, j: (b * nt + j, 0)) if blk[0] != TOP_K
                                          else (lambda b, j: (0, b * nt + j)))
    out_shape = [jax.ShapeDtypeStruct((bsz, t, d), _F32)] + r_shapes + [
        jax.ShapeDtypeStruct((bsz, CONV_A_HALO, d), _F32),
        jax.ShapeDtypeStruct((bsz, 1, d), _F32),
        jax.ShapeDtypeStruct((bsz, CONF_HALO, d), _F32),
    ]
    out_specs = [pl.BlockSpec((None, tt, d), lambda b, j: (b, j, 0))] + [tile_major(blk) for blk in r_blocks] + [
        pl.BlockSpec((None, CONV_A_HALO, d), lambda b, j: (b, 0, 0)),
        pl.BlockSpec((None, 1, d), lambda b, j: (b, 0, 0)),
        pl.BlockSpec((None, CONF_HALO, d), lambda b, j: (b, 0, 0)),
    ]
    scratch = [
        pltpu.VMEM((d // LANE, tt + CONV_A_HALO, LANE), _F32),
        pltpu.VMEM((d // LANE, tt + CONF_HALO, LANE), _F32),
        pltpu.VMEM((tt, d), _F32), pltpu.VMEM((tt, d), _F32), pltpu.VMEM((tt, d), _F32),
        pltpu.VMEM((tt, d), _F32), pltpu.VMEM((tt, d), _F32),
        pltpu.VMEM((SUBLANE, d), _F32),
    ]
    return pl.pallas_call(
        _prompt_mixer_body,
        grid=(bsz, nt),
        in_specs=[pl.BlockSpec((None, tt, d), lambda b, j: (b, j, 0))] + [_const_spec(w.shape) for w in w_args],
        out_specs=out_specs,
        out_shape=out_shape,
        scratch_shapes=scratch,
        compiler_params=pltpu.CompilerParams(dimension_semantics=("arbitrary", "arbitrary"),
                                             vmem_limit_bytes=VMEM_LIMIT),
        name="prompt_mixer",
    )(x, *w_args)


def _sample_mixer_body(nb, x_ref, sca_ref, sh_ref, scf_ref, wst_ref,
                       gmix_ref, win_ref, wca_ref, bca_ref, wra_ref, bra_ref, wrx_ref, brx_ref,
                       lam_ref, wdw_ref, bdw_ref, gln_ref, bln_ref, wpw_ref, bpw_ref, wout_ref,
                       gffn_ref, wrt_ref, br_ref,
                       h_ref, xs_ref, pos_ref, prob_ref, cnt_ref, v_out_ref, sh_out_ref, xa_out_ref,
                       s0, s1, s2, s3, s4):
    n, d = x_ref.shape
    steps = n // nb
    x = x_ref[...]
    hn = _rms(x, gmix_ref[...]).astype(_BF)

    xa = _dot(hn, win_ref[:, 0:d])
    xa_out_ref[...] = xa
    kw = wca_ref.shape[0]

    def xcat_a(slab):
        if slab < kw - 1:
            return sca_ref[slab * nb:(slab + 1) * nb, :]
        s = slab - (kw - 1)
        return xa[s * nb:(s + 1) * nb, :]

    for t in range(steps):
        acc = bca_ref[...] + jnp.zeros((nb, d), _F32)
        for k in range(kw):
            acc = acc + wca_ref[k:k + 1, :] * xcat_a(t + k)
        s4[t * nb:(t + 1) * nb, :] = acc
    xcb = s4[...].astype(_BF)
    _block_diag(xcb, wra_ref, bra_ref, s0)
    _block_diag(xcb, wrx_ref, brx_ref, s1)
    _rglru_coeffs(s4, s0, s1, lam_ref, s2, s3)
    h = sh_ref[...]
    for t in range(steps):
        h = s2[t * nb:(t + 1) * nb, :] * h + s3[t * nb:(t + 1) * nb, :]
        s0[t * nb:(t + 1) * nb, :] = h
    sh_out_ref[...] = h

    v = _dot(hn, win_ref[:, d:2 * d]) * _sigmoid(_dot(hn, win_ref[:, 2 * d:3 * d]))
    v_out_ref[...] = v
    kc = wdw_ref.shape[0]
    sub = 2 * SUBLANE
    for t in range(steps):
        for q in range(0, nb, sub):
            part = jnp.sum(scf_ref[q:q + sub, :, :] * wst_ref[t][None, :, :], axis=1)
            s1[t * nb + q:t * nb + q + sub, :] = part + bdw_ref[...]
        acc = s1[t * nb:(t + 1) * nb, :]
        for s in range(t + 1):
            k = kc - 1 - t + s
            acc = acc + wdw_ref[k:k + 1, :] * v[s * nb:(s + 1) * nb, :]
        s1[t * nb:(t + 1) * nb, :] = acc
    sl =_layernorm_silu(s1[...], gln_ref[...], bln_ref[...])
    yb = _dot(sl.astype(_BF), wpw_ref[...]) + bpw_ref[...]

    ga = _dot(hn, win_ref[:, 3 * d:4 * d])
    gb = _dot(hn, win_ref[:, 4 * d:5 * d])
    xn_b = _tail(x, s0[...], yb, ga, gb, wout_ref, gffn_ref, h_ref)
    _route_and_sort(xn_b, wrt_ref, br_ref, pos_ref, prob_ref, cnt_ref, xs_ref)


def _sample_mixer(x_cm, sca_cm, sh, scf, wts, cb, steps):
    n, d = x_cm.shape
    rows = cb * steps
    n_chunks = n // rows
    ne = wts["wrt"].shape[0]
    kw = wts["wca"].shape[0]
    kc = wts["wdw"].shape[0]
    w_args = [wts[n_] for n_ in _W_NAMES]
    chunk = lambda r: pl.BlockSpec((r, d), lambda i: (i, 0))
    r_shapes, r_blocks = _routing_outputs(n_chunks, rows, d, ne)
    tile_major = lambda blk: pl.BlockSpec(blk, (lambda i: (i, 0)) if blk[0] != TOP_K else (lambda i: (0, i)))
    conf_state = pl.BlockSpec((cb, kc - 1, d), lambda i: (i, 0, 0), pipeline_mode=pl.Buffered(1))
    wdw = wts["wdw"]
    w_state = jnp.stack([jnp.concatenate([jnp.zeros((t, d), _F32), wdw[0:kc - 1 - t]], axis=0)
                         for t in range(steps)])
    in_specs = [chunk(rows), chunk((kw - 1) * cb), chunk(cb), conf_state, _const_spec(w_state.shape)] + \
        [_const_spec(w.shape) for w in w_args]
    out_shape = [jax.ShapeDtypeStruct((n, d), _F32)] + r_shapes + [
        jax.ShapeDtypeStruct((n, d), _F32),
        jax.ShapeDtypeStruct(sh.shape, _F32),
        jax.ShapeDtypeStruct((n, d), _F32),
    ]
    out_specs = [chunk(rows)] + [tile_major(blk) for blk in r_blocks] + [chunk(rows), chunk(cb), chunk(rows)]
    scratch = [pltpu.VMEM((rows, d), _F32)] * 5
    return pl.pallas_call(
        functools.partial(_sample_mixer_body, cb),
        grid=(n_chunks,),
        in_specs=in_specs,
        out_specs=out_specs,
        out_shape=out_shape,
        scratch_shapes=scratch,
        compiler_params=pltpu.CompilerParams(dimension_semantics=("arbitrary",),
                                             vmem_limit_bytes=VMEM_LIMIT),
        name="sample_mixer",
    )(x_cm, sca_cm, sh, scf, w_state, *w_args)


def _expert_ffn_body(ntp,
                     te_ref, nu_ref, valid_ref, ilo_ref, ihi_ref, psrc_ref, poff_ref, pn_ref, tail_ref,
                     xsp_ref, xss_ref, wg_ref, bg_ref, wu_ref, bu_ref, wd_ref, bd_ref,
                     ysp_ref, yss_ref,
                     lhs, obuf, zbuf, wg_b, wu_b, wd_b, gsem, ssem, zsem):
    j = pl.program_id(0)
    nu = nu_ref[0]
    nt_all = tail_ref.shape[0] // 2
    tm = lhs.shape[1]

    def for_pieces(step, fn):
        row0 = step * nt_all

        def run(group, first, last):
            def body(i, c):
                n = pn_ref[row0 + i]

                @pl.when(n > 0)
                def _():
                    fn(group, pl.multiple_of(psrc_ref[row0 + i], SEG_ALIGN),
                       pl.multiple_of(poff_ref[row0 + i], SEG_ALIGN), pl.multiple_of(n, SEG_ALIGN))

                return c

            lax.fori_loop(first, last, body, 0)

        run(0, ilo_ref[step], jnp.minimum(ihi_ref[step], ntp))
        run(1, jnp.maximum(ilo_ref[step], ntp), ihi_ref[step])

    def gather_start(step, slot):
        def fn(group, row, off, n):
            src = (xsp_ref, xss_ref)[group]
            pltpu.make_async_copy(src.at[pl.ds(row, n), :], lhs.at[slot, pl.ds(off, n), :], gsem.at[slot]).start()
        for_pieces(step, fn)

    def scatter_start(step, slot):
        def fn(group, row, off, n):
            dst = (ysp_ref, yss_ref)[group]
            pltpu.make_async_copy(obuf.at[slot, pl.ds(off, n), :], dst.at[pl.ds(row, n), :], ssem.at[slot]).start()
        for_pieces(step, fn)

    def gather_wait(step, slot):
        n = pl.multiple_of(valid_ref[step], SEG_ALIGN)
        pltpu.make_async_copy(xsp_ref.at[pl.ds(0, n), :], lhs.at[slot, pl.ds(0, n), :], gsem.at[slot]).wait()

    def scatter_wait(step, slot):
        n = pl.multiple_of(valid_ref[step], SEG_ALIGN)
        pltpu.make_async_copy(obuf.at[slot, pl.ds(0, n), :], ysp_ref.at[pl.ds(0, n), :], ssem.at[slot]).wait()

    def zero_tails(do_wait):
        def run(group, first, last):
            def one(i, c):
                row = pl.multiple_of(tail_ref[2 * i], SEG_ALIGN)
                n = pl.multiple_of(tail_ref[2 * i + 1], SEG_ALIGN)

                @pl.when(n > 0)
                def _():
                    dst = (ysp_ref, yss_ref)[group]
                    cp = pltpu.make_async_copy(zbuf.at[pl.ds(0, n), :], dst.at[pl.ds(row, n), :], zsem)
                    cp.wait() if do_wait else cp.start()

                return c

            lax.fori_loop(first, last, one, 0)

        run(0, 0, ntp)
        run(1, ntp, nt_all)

    @pl.when(j == 0)
    def _():
        lhs[...] = jnp.zeros_like(lhs)
        zbuf[...] = jnp.zeros_like(zbuf)
        zero_tails(False)
        zero_tails(True)
        gather_start(0, 0)

    @pl.when(j + 1 < nu)
    def _():
        gather_start(j + 1, (j + 1) % 2)

    slot = j % 2

    @pl.when((j >= 2) & (j - 2 < nu))
    def _():
        scatter_wait(j - 2, slot)

    def compute(rows):
        x = lhs[slot, 0:rows, :].astype(_BF)
        g = jnp.minimum(_dot(x, wg_b[...]) + bg_ref[...], SWIGLU_LIMIT)
        u = jnp.clip(_dot(x, wu_b[...]) + bu_ref[...], -SWIGLU_LIMIT, SWIGLU_LIMIT)
        hdn = (u + 1.0) * (g * _sigmoid(SWIGLU_ALPHA * g))
        obuf[slot, 0:rows, :] = _dot(hdn.astype(_BF), wd_b[...]) + bd_ref[...]

    @pl.when(j < nu)
    def _():
        gather_wait(j, slot)
        prev = te_ref[jnp.maximum(j - 1, 0)]

        @pl.when((j == 0) | (te_ref[j] != prev))
        def _():
            wg_b[...] = wg_ref[...].astype(_BF)
            wu_b[...] = wu_ref[...].astype(_BF)
            wd_b[...] = wd_ref[...].astype(_BF)

        @pl.when(valid_ref[j] > tm // 2)
        def _():
            compute(tm)

        @pl.when(valid_ref[j] <= tm // 2)
        def _():
            compute(tm // 2)

        scatter_start(j, slot)


def _expert_ffn(meta, xs_p, xs_s, wg, bg, wu, bu, wd, bd, tm, slr, n_tiles):
    ne, d, de = wg.shape
    ntp = xs_p.shape[0] // slr
    n_meta = len(meta)
    wspec = lambda a, b_: pl.BlockSpec((None, a, b_), lambda j, te, *_: (te[j], 0, 0))
    hbm = pl.BlockSpec(memory_space=pl.ANY)
    tail_rows = ne * SEG_ALIGN
    return pl.pallas_call(
        functools.partial(_expert_ffn_body, ntp),
        grid_spec=pltpu.PrefetchScalarGridSpec(
            num_scalar_prefetch=n_meta,
            grid=(n_tiles,),
            in_specs=[hbm, hbm, wspec(d, de), wspec(1, de), wspec(d, de), wspec(1, de), wspec(de, d), wspec(1, d)],
            out_specs=[hbm, hbm],
            scratch_shapes=[
                pltpu.VMEM((2, tm, d), _F32), pltpu.VMEM((2, tm, d), _F32),
                pltpu.VMEM((tail_rows, d), _F32),
                pltpu.VMEM((d, de), _BF), pltpu.VMEM((d, de), _BF), pltpu.VMEM((de, d), _BF),
                pltpu.SemaphoreType.DMA((2,)), pltpu.SemaphoreType.DMA((2,)), pltpu.SemaphoreType.DMA(()),
            ],
        ),
        out_shape=[jax.ShapeDtypeStruct(xs_p.shape, _F32), jax.ShapeDtypeStruct(xs_s.shape, _F32)],
        compiler_params=pltpu.CompilerParams(dimension_semantics=("arbitrary",),
                                             vmem_limit_bytes=VMEM_LIMIT),
        name="expert_ffn",
    )(*meta, xs_p, xs_s, wg, bg, wu, bu, wd, bd)


def _combine_body(ys_ref, h_ref, pos_ref, prob_ref, gfin_ref, y_ref):
    tt, d = h_ref.shape
    slr = ys_ref.shape[0]
    ciota = lax.broadcasted_iota(jnp.int32, (tt, slr), 1).astype(_F32)
    w = jnp.zeros((tt, slr), _F32)
    for k in range(TOP_K):
        w = jnp.where(ciota == pos_ref[:, k:k + 1], prob_ref[:, k:k + 1], w)
    out = h_ref[...] + _dot(w.astype(_BF), ys_ref[...].astype(_BF))
    y_ref[...] = _rms(out, gfin_ref[...])


def _combine(ys, h, pos_t, prob_t, g_final, tt, slr):
    n, d = h.shape
    return pl.pallas_call(
        _combine_body,
        grid=(n // tt,),
        in_specs=[pl.BlockSpec((slr, d), lambda i: (i, 0)),
                  pl.BlockSpec((tt, d), lambda i: (i, 0)),
                  pl.BlockSpec((tt, TOP_K), lambda i: (i, 0)),
                  pl.BlockSpec((tt, TOP_K), lambda i: (i, 0)),
                  pl.BlockSpec((1, d), lambda i: (0, 0))],
        out_specs=pl.BlockSpec((tt, d), lambda i: (i, 0)),
        out_shape=jax.ShapeDtypeStruct((n, d), _F32),
        compiler_params=pltpu.CompilerParams(dimension_semantics=("arbitrary",),
                                             vmem_limit_bytes=VMEM_LIMIT),
        name="combine",
    )(ys, h, pos_t, prob_t, g_final)


def _tile_sizes(seq, nb, steps):
    tt = min(256, seq)
    return tt, tt // steps, 512


def _row_tile_tables(counts, tm, n_tiles, ntp, slr):
    nt_all, ne = counts.shape
    cp = (counts + (SEG_ALIGN - 1)) // SEG_ALIGN * SEG_ALIGN
    lo = jnp.cumsum(cp, axis=1) - cp
    base = jnp.cumsum(cp, axis=0) - cp
    used = jnp.sum(cp, axis=1)
    total = jnp.sum(cp, axis=0)
    tiles_e = (total + tm - 1) // tm
    tile_end = jnp.cumsum(tiles_e)
    first_tile = tile_end - tiles_e
    ids = jnp.arange(n_tiles, dtype=jnp.int32)
    te = jnp.minimum(jnp.sum((tile_end[None, :] <= ids[:, None]).astype(jnp.int32), axis=1), ne - 1)
    r0 = (ids - first_tile[te]) * tm
    valid = jnp.clip(total[te] - r0, 0, tm)
    seg_start = base.T[te]
    seg_end = seg_start + cp.T[te]
    r1 = r0 + valid
    ilo = jnp.sum((seg_end <= r0[:, None]).astype(jnp.int32), axis=1)
    ihi = jnp.sum((seg_start < r1[:, None]).astype(jnp.int32), axis=1)
    first = jnp.maximum(seg_start, r0[:, None])
    p_n = jnp.maximum(jnp.minimum(seg_end, r1[:, None]) - first, 0)
    tile_row0 = jnp.where(jnp.arange(nt_all) < ntp, jnp.arange(nt_all), jnp.arange(nt_all) - ntp) * slr
    p_src = tile_row0[None, :] + lo.T[te] + (first - seg_start)
    p_off = first - r0[:, None]
    tails = jnp.stack([tile_row0 + used, slr - used], axis=1)
    i32 = lambda a: a.astype(jnp.int32).reshape(-1)
    return [i32(te), i32(tile_end[-1:]), i32(valid), i32(ilo), i32(ihi), i32(p_src), i32(p_off), i32(p_n),
            i32(tails)]


def kernel(x_prompt, x_sample, state_conv_a, state_h, state_conf, g_mix, w_in, w_conv_a, b_conv_a,
           w_rg_a, b_rg_a, w_rg_x, b_rg_x, rg_lambda, w_conf_dw, b_conf_dw, g_conf_ln, b_conf_ln,
           w_conf_pw, b_conf_pw, w_out, g_ffn, w_router, b_router, w_gate, b_gate, w_up, b_up,
           w_down, b_down, g_final):
    depth = g_mix.shape[0]
    assert depth == 1
    bsz, seq, d = x_prompt.shape
    nb, steps, _ = x_sample.shape
    ne = w_router.shape[-1]
    kw = w_conv_a.shape[1]
    kc = w_conf_dw.shape[1]
    n_p = bsz * seq
    n_s = nb * steps
    tt, cb, tm = _tile_sizes(seq, nb, steps)
    assert seq % tt == 0 and nb % cb == 0 and cb * steps == tt
    slr = _sorted_rows(tt, ne)
    ntp, nts = n_p // tt, n_s // tt

    row = lambda a: a[0].reshape(1, -1)
    wts = dict(
        gmix=row(g_mix), win=w_in[0].astype(_BF), wca=w_conv_a[0], bca=row(b_conv_a),
        wra=w_rg_a[0].astype(_BF), bra=row(b_rg_a), wrx=w_rg_x[0].astype(_BF), brx=row(b_rg_x),
        lam=row(rg_lambda), wdw=w_conf_dw[0], bdw=row(b_conf_dw), gln=row(g_conf_ln), bln=row(b_conf_ln),
        wpw=w_conf_pw[0].astype(_BF), bpw=row(b_conf_pw), wout=w_out[0].astype(_BF), gffn=row(g_ffn),
        wrt=w_router[0].T.astype(_BF), br=b_router[0].reshape(ne, 1),
    )

    (h_p, xs_p, pos_p, prob_p, cnt_p, pca, ph, pcf) = _prompt_mixer(x_prompt, wts, tt)
    to_cm = lambda a: jnp.swapaxes(a.reshape(nb // cb, cb, a.shape[1], d), 1, 2).reshape(-1, d)
    (h_s, xs_s, pos_s, prob_s, cnt_s, v_s, sh_new, xa_s) = _sample_mixer(
        to_cm(x_sample), to_cm(state_conv_a[0]), state_h[0], state_conf[0], wts, cb, steps)

    counts = jnp.concatenate([cnt_p[:, 0].reshape(ntp, ne), cnt_s[:, 0].reshape(nts, ne)], axis=0)
    n_tiles = ((ntp + nts) * slr) // tm + ne + 2
    meta = _row_tile_tables(counts.astype(jnp.int32), tm, n_tiles, ntp, slr)

    ys_p, ys_s = _expert_ffn(meta, xs_p, xs_s, w_gate[0], b_gate[0].reshape(ne, 1, -1), w_up[0],
                             b_up[0].reshape(ne, 1, -1), w_down[0], b_down[0].reshape(ne, 1, -1), tm, slr, n_tiles)
    gfin = g_final.reshape(1, d)
    y_p = _combine(ys_p, h_p.reshape(n_p, d), pos_p.T, prob_p.T, gfin, tt, slr)
    y_s = _combine(ys_s, h_s, pos_s.T, prob_s.T, gfin, tt, slr)

    nat = lambda a: jnp.swapaxes(a.reshape(nb // cb, steps, cb, d), 1, 2).reshape(nb, steps, d)
    y_prompt = y_p.reshape(bsz, seq, d)
    y_sample = nat(y_s)
    p_conv_a = pca[:, CONV_A_HALO - (kw - 1):, :][None]
    p_h = ph.reshape(1, bsz, d)
    p_conf = pcf[:, CONF_HALO - (kc - 1):, :][None]
    s_conv_a = jnp.concatenate([state_conv_a[0], nat(xa_s)], axis=1)[:, -(kw - 1):, :][None]
    s_h = sh_new[None]
    s_conf = jnp.concatenate([state_conf[0], nat(v_s)], axis=1)[:, -(kc - 1):, :][None]
    return (y_prompt, y_sample, p_conv_a, p_h, p_conf, s_conv_a, s_h, s_conf)
```

```python
import functools

import jax
import jax.numpy as jnp
from jax import lax
from jax.experimental import pallas as pl
from jax.experimental.pallas import tpu as pltpu

EPS = 1e-6
RG_C = 8.0
SWIGLU_LIMIT = 7.0
SWIGLU_ALPHA = 1.702
TOP_K = 4
LANE = 128
SUBLANE = 8
CONV_A_HALO = 8
CONF_HALO = 32
TAP_GROUP = 8
ROW_QUANTUM = 256
SEG_ALIGN = SUBLANE
VMEM_LIMIT = 56 * 1024 * 1024

_BF = jnp.bfloat16
_F32 = jnp.float32


def _const_spec(shape):
    nd = len(shape)
    return pl.BlockSpec(shape, lambda *_: (0,) * nd, pipeline_mode=pl.Buffered(1))


def _sorted_rows(tt, ne):
    return TOP_K * tt + ne * SEG_ALIGN


def _rms(x, g):
    return x * lax.rsqrt(jnp.mean(x * x, axis=-1, keepdims=True) + EPS) * g


def _sigmoid(x):
    return jax.nn.sigmoid(x)


def _dot(a, b):
    return jnp.dot(a, b, preferred_element_type=_F32)


def _block_diag(xb, w_ref, b_ref, out_ref):
    nb, gb, _ = w_ref.shape
    for n in range(nb):
        cs = slice(n * gb, (n + 1) * gb)
        out_ref[:, cs] = _dot(xb[:, cs], w_ref[n]) + b_ref[:, cs]


def _rglru_coeffs(xc_ref, r_ref, i_ref, lam_ref, a_ref, u_ref):
    sp = jax.nn.softplus(-lam_ref[...])
    r = _sigmoid(r_ref[...])
    i = _sigmoid(i_ref[...])
    log_a = -RG_C * r * sp
    a = jnp.exp(log_a)
    a_ref[...] = a
    u_ref[...] = jnp.sqrt(-jnp.tanh(log_a) * (a * a + 1.0)) * (i * xc_ref[...])


def _layernorm_silu(vc, g, b):
    mu = jnp.mean(vc, axis=-1, keepdims=True)
    xc = vc - mu
    y = xc * lax.rsqrt(jnp.mean(xc * xc, axis=-1, keepdims=True) + EPS) * g + b
    return y * _sigmoid(y)


def _route_and_sort(xn_b, wrt_ref, br_ref, pos_ref, prob_ref, cnt_ref, xs_ref):
    ne = wrt_ref.shape[0]
    tt = xn_b.shape[0]
    slr = xs_ref.shape[0]
    logits = lax.dot_general(wrt_ref[...], xn_b, (((1,), (1,)), ((), ())),
                             preferred_element_type=_F32) + br_ref[...]
    eidx = lax.broadcasted_iota(jnp.int32, (ne, tt), 0).astype(_F32)
    work = logits
    vals, onehots = [], []
    for _ in range(TOP_K):
        m = jnp.max(work, axis=0, keepdims=True)
        pick = jnp.min(jnp.where(work == m, eidx, float(ne)), axis=0, keepdims=True)
        one = eidx == pick
        work = jnp.where(one, -jnp.inf, work)
        vals.append(m)
        onehots.append(one)
    exps = [jnp.exp(v - vals[0]) for v in vals]
    inv = 1.0 / (exps[0] + exps[1] + exps[2] + exps[3])
    sel = jnp.zeros((ne, tt), _F32)
    for one in onehots:
        sel = sel + one.astype(_F32)
    row = lax.broadcasted_iota(jnp.int32, (tt, tt), 0)
    col = lax.broadcasted_iota(jnp.int32, (tt, tt), 1)
    incl = _dot(sel.astype(_BF), (row <= col).astype(_BF))
    count = incl[:, tt - 1:tt]
    groups = jnp.floor((count + (SEG_ALIGN - 1)) * (1.0 / SEG_ALIGN))
    er = lax.broadcasted_iota(jnp.int32, (ne, ne), 0)
    ec = lax.broadcasted_iota(jnp.int32, (ne, ne), 1)
    lo = _dot((ec < er).astype(_BF), jnp.broadcast_to(groups, (ne, LANE)).astype(_BF))[:, 0:1] * SEG_ALIGN
    before = lo + (incl - sel)
    riota = lax.broadcasted_iota(jnp.int32, (slr, tt), 0).astype(_F32)
    perm = jnp.zeros((slr, tt), _F32)
    for k in range(TOP_K):
        pos = jnp.sum(jnp.where(onehots[k], before, 0.0), axis=0, keepdims=True)
        pos_ref[k:k + 1, :] = pos
        prob_ref[k:k + 1, :] = exps[k] * inv
        perm = jnp.where(riota == pos, 1.0, perm)
    cnt_ref[...] = jnp.broadcast_to(count, cnt_ref.shape)
    xs_ref[...] = _dot(perm.astype(_BF), xn_b)


def _tail(x, ya, yb, ga, gb, wout_ref, gffn_ref, h_ref):
    m = _sigmoid(ga) * ya + _sigmoid(gb) * yb
    h = x + _dot(m.astype(_BF), wout_ref[...])
    h_ref[...] = h
    return _rms(h, gffn_ref[...]).astype(_BF)


_W_NAMES = ["gmix", "win", "wca", "bca", "wra", "bra", "wrx", "brx", "lam", "wdw", "bdw", "gln", "bln",
            "wpw", "bpw", "wout", "gffn", "wrt", "br"]


def _causal_conv_tile(x, ext_ref, w_ref, b_ref, out_ref):
    tt, d = x.shape
    halo = ext_ref.shape[1] - tt
    kk = w_ref.shape[0]
    off = halo - (kk - 1)
    for c in range(d // LANE):
        cs = slice(c * LANE, (c + 1) * LANE)
        ext_ref[c, halo:halo + tt, :] = x[:, cs]
        bb = jnp.broadcast_to(b_ref[0:1, cs], (SUBLANE, LANE))
        for k0 in range(0, kk, TAP_GROUP):
            taps = range(k0, min(k0 + TAP_GROUP, kk))
            wb = {k: jnp.broadcast_to(w_ref[k:k + 1, cs], (SUBLANE, LANE)) for k in taps}
            for r0 in range(0, tt, SUBLANE):
                acc = bb if k0 == 0 else out_ref[r0:r0 + SUBLANE, cs]
                for k in taps:
                    acc = acc + wb[k] * ext_ref[c, r0 + off + k:r0 + off + k + SUBLANE, :]
                out_ref[r0:r0 + SUBLANE, cs] = acc
        ext_ref[c, 0:halo, :] = ext_ref[c, tt:tt + halo, :]


def _prompt_mixer_body(x_ref, gmix_ref, win_ref, wca_ref, bca_ref, wra_ref, bra_ref, wrx_ref, brx_ref,
                       lam_ref, wdw_ref, bdw_ref, gln_ref, bln_ref, wpw_ref, bpw_ref, wout_ref,
                       gffn_ref, wrt_ref, br_ref,
                       h_ref, xs_ref, pos_ref, prob_ref, cnt_ref, pca_ref, ph_ref, pcf_ref,
                       xa_ext, v_ext, s0, s1, s2, s3, s4, hcar):
    tt, d = x_ref.shape
    j = pl.program_id(1)

    @pl.when(j == 0)
    def _():
        hcar[...] = jnp.zeros_like(hcar)
        xa_ext[:, 0:CONV_A_HALO, :] = jnp.zeros((d // LANE, CONV_A_HALO, LANE), _F32)
        v_ext[:, 0:CONF_HALO, :] = jnp.zeros((d // LANE, CONF_HALO, LANE), _F32)

    x = x_ref[...]
    hn = _rms(x, gmix_ref[...]).astype(_BF)

    xa = _dot(hn, win_ref[:, 0:d])
    _causal_conv_tile(xa, xa_ext, wca_ref, bca_ref, s4)
    pca_ref[...] = xa[tt - CONV_A_HALO:tt, :]
    xcb = s4[...].astype(_BF)
    _block_diag(xcb, wra_ref, bra_ref, s0)
    _block_diag(xcb, wrx_ref, brx_ref, s1)
    _rglru_coeffs(s4, s0, s1, lam_ref, s2, s3)

    ri = lax.broadcasted_iota(jnp.int32, (SUBLANE, d), 0)

    h_last = hcar[...]
    for r0 in range(0, tt, SUBLANE):
        a = s2[r0:r0 + SUBLANE, :]
        u = s3[r0:r0 + SUBLANE, :]
        for s in (1, 2, 4):
            keep = ri >= s
            a_sh = jnp.where(keep, pltpu.roll(a, s, axis=0), 1.0)
            u_sh = jnp.where(keep, pltpu.roll(u, s, axis=0), 0.0)
            u = a * u_sh + u
            a = a * a_sh
        hh = a * h_last + u
        s0[r0:r0 + SUBLANE, :] = hh
        h_last = jnp.broadcast_to(hh[SUBLANE - 1:SUBLANE, :], (SUBLANE, d))
    hcar[...] = h_last
    ph_ref[...] = h_last[0:1, :]

    v = _dot(hn, win_ref[:, d:2 * d]) * _sigmoid(_dot(hn, win_ref[:, 2 * d:3 * d]))
    pcf_ref[...] = v[tt - CONF_HALO:tt, :]
    _causal_conv_tile(v, v_ext, wdw_ref, bdw_ref, s1)
    sl = _layernorm_silu(s1[...], gln_ref[...], bln_ref[...])
    yb = _dot(sl.astype(_BF), wpw_ref[...]) + bpw_ref[...]

    ga = _dot(hn, win_ref[:, 3 * d:4 * d])
    gb = _dot(hn, win_ref[:, 4 * d:5 * d])
    xn_b = _tail(x, s0[...], yb, ga, gb, wout_ref, gffn_ref, h_ref)
    _route_and_sort(xn_b, wrt_ref, br_ref, pos_ref, prob_ref, cnt_ref, xs_ref)


def _routing_outputs(n_tiles, tt, d, ne):
    slr = _sorted_rows(tt, ne)
    shapes = [
        jax.ShapeDtypeStruct((n_tiles * slr, d), _F32),
        jax.ShapeDtypeStruct((TOP_K, n_tiles * tt), _F32),
        jax.ShapeDtypeStruct((TOP_K, n_tiles * tt), _F32),
        jax.ShapeDtypeStruct((n_tiles * ne, LANE), _F32),
    ]
    blocks = [(slr, d), (TOP_K, tt), (TOP_K, tt), (ne, LANE)]
    return shapes, blocks


def _prompt_mixer(x, wts, tt):
    bsz, t, d = x.shape
    nt = t // tt
    ne = wts["wrt"].shape[0]
    w_args = [wts[n] for n in _W_NAMES]
    r_shapes, r_blocks = _routing_outputs(bsz * nt, tt, d, ne)
    tile_major = lambda blk: pl.BlockSpec(blk, (lambda b, j: (b * nt + j, 0)) if blk[0] != TOP_K
                                          else (lambda b, j: (0, b * nt + j)))
    out_shape = [jax.ShapeDtypeStruct((bsz, t, d), _F32)] + r_shapes + [
        jax.ShapeDtypeStruct((bsz, CONV_A_HALO, d), _F32),
        jax.ShapeDtypeStruct((bsz, 1, d), _F32),
        jax.ShapeDtypeStruct((bsz, CONF_HALO, d), _F32),
    ]
    out_specs = [pl.BlockSpec((None, tt, d), lambda b, j: (b, j, 0))] + [tile_major(blk) for blk in r_blocks] + [
        pl.BlockSpec((None, CONV_A_HALO, d), lambda b, j: (b, 0, 0)),
        pl.BlockSpec((None, 1, d), lambda b, j: (b, 0, 0)),
        pl.BlockSpec((None, CONF_HALO, d), lambda b, j: (b, 0, 0)),
    ]
    scratch = [
        pltpu.VMEM((d // LANE, tt + CONV_A_HALO, LANE), _F32),
        pltpu.VMEM((d // LANE, tt + CONF_HALO, LANE), _F32),
        pltpu.VMEM((tt, d), _F32), pltpu.VMEM((tt, d), _F32), pltpu.VMEM((tt, d), _F32),
        pltpu.VMEM((tt, d), _F32), pltpu.VMEM((tt, d), _F32),
        pltpu.VMEM((SUBLANE, d), _F32),
    ]
    return pl.pallas_call(
        _prompt_mixer_body,
        grid=(bsz, nt),
        in_specs=[pl.BlockSpec((None, tt, d), lambda b, j: (b, j, 0))] + [_const_spec(w.shape) for w in w_args],
        out_specs=out_specs,
        out_shape=out_shape,
        scratch_shapes=scratch,
        compiler_params=pltpu.CompilerParams(dimension_semantics=("arbitrary", "arbitrary"),
                                             vmem_limit_bytes=VMEM_LIMIT),
        name="prompt_mixer",
    )(x, *w_args)


def _sample_mixer_body(nb, x_ref, sca_ref, sh_ref, scf_ref,
                       gmix_ref, win_ref, wca_ref, bca_ref, wra_ref, bra_ref, wrx_ref, brx_ref,
                       lam_ref, wdw_ref, bdw_ref, gln_ref, bln_ref, wpw_ref, bpw_ref, wout_ref,
                       gffn_ref, wrt_ref, br_ref,
                       h_ref, xs_ref, pos_ref, prob_ref, cnt_ref, v_out_ref, sh_out_ref, xa_out_ref,
                       s0, s1, s2, s3, s4):
    n, d = x_ref.shape
    steps = n // nb
    x = x_ref[...]
    hn = _rms(x, gmix_ref[...]).astype(_BF)

    xa = _dot(hn, win_ref[:, 0:d])
    xa_out_ref[...] = xa
    kw = wca_ref.shape[0]

    def xcat_a(slab):
        if slab < kw - 1:
            return sca_ref[slab * nb:(slab + 1) * nb, :]
        s = slab - (kw - 1)
        return xa[s * nb:(s + 1) * nb, :]

    for t in range(steps):
        acc = bca_ref[...] + jnp.zeros((nb, d), _F32)
        for k in range(kw):
            acc = acc + wca_ref[k:k + 1, :] * xcat_a(t + k)
        s4[t * nb:(t + 1) * nb, :] = acc
    xcb = s4[...].astype(_BF)
    _block_diag(xcb, wra_ref, bra_ref, s0)
    _block_diag(xcb, wrx_ref, brx_ref, s1)
    _rglru_coeffs(s4, s0, s1, lam_ref, s2, s3)
    h = sh_ref[...]
    for t in range(steps):
        h = s2[t * nb:(t + 1) * nb, :] * h + s3[t * nb:(t + 1) * nb, :]
        s0[t * nb:(t + 1) * nb, :] = h
    sh_out_ref[...] = h

    v = _dot(hn, win_ref[:, d:2 * d]) * _sigmoid(_dot(hn, win_ref[:, 2 * d:3 * d]))
    v_out_ref[...] = v
    kc = wdw_ref.shape[0]

    def xcat_v(slab):
        if slab < kc - 1:
            return scf_ref[slab * nb:(slab + 1) * nb, :]
        s = slab - (kc - 1)
        return v[s * nb:(s + 1) * nb, :]

    for t in range(steps):
        acc = bdw_ref[...] + jnp.zeros((nb, d), _F32)
        for k in range(kc):
            acc = acc + wdw_ref[k:k + 1, :] * xcat_v(t + k)
        s1[t * nb:(t + 1) * nb, :] = acc
    sl = _layernorm_silu(s1[...], gln_ref[...], bln_ref[...])
    yb = _dot(sl.astype(_BF), wpw_ref[...]) + bpw_ref[...]

    ga = _dot(hn, win_ref[:, 3 * d:4 * d])
    gb = _dot(hn, win_ref[:, 4 * d:5 * d])
    xn_b = _tail(x, s0[...], yb, ga, gb, wout_ref, gffn_ref, h_ref)
    _route_and_sort(xn_b, wrt_ref, br_ref, pos_ref, prob_ref, cnt_ref, xs_ref)


def _sample_mixer(x_cm, sca_cm, sh, scf_cm, wts, cb, steps):
    n, d = x_cm.shape
    rows = cb * steps
    n_chunks = n // rows
    ne = wts["wrt"].shape[0]
    kw = wts["wca"].shape[0]
    kc = wts["wdw"].shape[0]
    w_args = [wts[n_] for n_ in _W_NAMES]
    chunk = lambda r: pl.BlockSpec((r, d), lambda i: (i, 0))
    r_shapes, r_blocks = _routing_outputs(n_chunks, rows, d, ne)
    tile_major = lambda blk: pl.BlockSpec(blk, (lambda i: (i, 0)) if blk[0] != TOP_K else (lambda i: (0, i)))
    conf_state = pl.BlockSpec(((kc - 1) * cb, d), lambda i: (i, 0), pipeline_mode=pl.Buffered(1))
    in_specs = [chunk(rows), chunk((kw - 1) * cb), chunk(cb), conf_state] + \
        [_const_spec(w.shape) for w in w_args]
    out_shape = [jax.ShapeDtypeStruct((n, d), _F32)] + r_shapes + [
        jax.ShapeDtypeStruct((n, d), _F32),
        jax.ShapeDtypeStruct(sh.shape, _F32),
        jax.ShapeDtypeStruct((n, d), _F32),
    ]
    out_specs = [chunk(rows)] + [tile_major(blk) for blk in r_blocks] + [chunk(rows), chunk(cb), chunk(rows)]
    scratch = [pltpu.VMEM((rows, d), _F32)] * 5
    return pl.pallas_call(
        functools.partial(_sample_mixer_body, cb),
        grid=(n_chunks,),
        in_specs=in_specs,
        out_specs=out_specs,
        out_shape=out_shape,
        scratch_shapes=scratch,
        compiler_params=pltpu.CompilerParams(dimension_semantics=("arbitrary",),
                                             vmem_limit_bytes=VMEM_LIMIT),
        name="sample_mixer",
    )(x_cm, sca_cm, sh, scf_cm, *w_args)


def _expert_ffn_body(ntp,
                     te_ref, nu_ref, valid_ref, ilo_ref, ihi_ref, psrc_ref, poff_ref, pn_ref, tail_ref,
                     xsp_ref, xss_ref, wg_ref, bg_ref, wu_ref, bu_ref, wd_ref, bd_ref,
                     ysp_ref, yss_ref,
                     lhs, obuf, zbuf, wg_b, wu_b, wd_b, gsem, ssem, zsem):
    j = pl.program_id(0)
    nu = nu_ref[0]
    nt_all = tail_ref.shape[0] // 2
    tm = lhs.shape[1]

    def for_pieces(step, fn):
        row0 = step * nt_all

        def run(group, first, last):
            def body(i, c):
                n = pn_ref[row0 + i]

                @pl.when(n > 0)
                def _():
                    fn(group, pl.multiple_of(psrc_ref[row0 + i], SEG_ALIGN),
                       pl.multiple_of(poff_ref[row0 + i], SEG_ALIGN), pl.multiple_of(n, SEG_ALIGN))

                return c

            lax.fori_loop(first, last, body, 0)

        run(0, ilo_ref[step], jnp.minimum(ihi_ref[step], ntp))
        run(1, jnp.maximum(ilo_ref[step], ntp), ihi_ref[step])

    def gather_start(step, slot):
        def fn(group, row, off, n):
            src = (xsp_ref, xss_ref)[group]
            pltpu.make_async_copy(src.at[pl.ds(row, n), :], lhs.at[slot, pl.ds(off, n), :], gsem.at[slot]).start()
        for_pieces(step, fn)

    def scatter_start(step, slot):
        def fn(group, row, off, n):
            dst = (ysp_ref, yss_ref)[group]
            pltpu.make_async_copy(obuf.at[slot, pl.ds(off, n), :], dst.at[pl.ds(row, n), :], ssem.at[slot]).start()
        for_pieces(step, fn)

    def gather_wait(step, slot):
        n = pl.multiple_of(valid_ref[step], SEG_ALIGN)
        pltpu.make_async_copy(xsp_ref.at[pl.ds(0, n), :], lhs.at[slot, pl.ds(0, n), :], gsem.at[slot]).wait()

    def scatter_wait(step, slot):
        n = pl.multiple_of(valid_ref[step], SEG_ALIGN)
        pltpu.make_async_copy(obuf.at[slot, pl.ds(0, n), :], ysp_ref.at[pl.ds(0, n), :], ssem.at[slot]).wait()

    def zero_tails(do_wait):
        def run(group, first, last):
            def one(i, c):
                row = pl.multiple_of(tail_ref[2 * i], SEG_ALIGN)
                n = pl.multiple_of(tail_ref[2 * i + 1], SEG_ALIGN)

                @pl.when(n > 0)
                def _():
                    dst = (ysp_ref, yss_ref)[group]
                    cp = pltpu.make_async_copy(zbuf.at[pl.ds(0, n), :], dst.at[pl.ds(row, n), :], zsem)
                    cp.wait() if do_wait else cp.start()

                return c

            lax.fori_loop(first, last, one, 0)

        run(0, 0, ntp)
        run(1, ntp, nt_all)

    @pl.when(j == 0)
    def _():
        lhs[...] = jnp.zeros_like(lhs)
        zbuf[...] = jnp.zeros_like(zbuf)
        zero_tails(False)
        zero_tails(True)
        gather_start(0, 0)

    @pl.when(j + 1 < nu)
    def _():
        gather_start(j + 1, (j + 1) % 2)

    slot = j % 2

    @pl.when((j >= 2) & (j - 2 < nu))
    def _():
        scatter_wait(j - 2, slot)

    def compute(rows):
        x = lhs[slot, 0:rows, :].astype(_BF)
        g = jnp.minimum(_dot(x, wg_b[...]) + bg_ref[...], SWIGLU_LIMIT)
        u = jnp.clip(_dot(x, wu_b[...]) + bu_ref[...], -SWIGLU_LIMIT, SWIGLU_LIMIT)
        hdn = (u + 1.0) * (g * _sigmoid(SWIGLU_ALPHA * g))
        obuf[slot, 0:rows, :] = _dot(hdn.astype(_BF), wd_b[...]) + bd_ref[...]

    @pl.when(j < nu)
    def _():
        gather_wait(j, slot)
        prev = te_ref[jnp.maximum(j - 1, 0)]

        @pl.when((j == 0) | (te_ref[j] != prev))
        def _():
            wg_b[...] = wg_ref[...].astype(_BF)
            wu_b[...] = wu_ref[...].astype(_BF)
            wd_b[...] = wd_ref[...].astype(_BF)

        for rows in range(ROW_QUANTUM, tm + 1, ROW_QUANTUM):
            @pl.when((valid_ref[j] > rows - ROW_QUANTUM) & (valid_ref[j] <= rows))
            def _(rows=rows):
                compute(rows)

        scatter_start(j, slot)


def _expert_ffn(meta, xs_p, xs_s, wg, bg, wu, bu, wd, bd, tm, slr, n_tiles):
    ne, d, de = wg.shape
    ntp = xs_p.shape[0] // slr
    n_meta = len(meta)
    wspec = lambda a, b_: pl.BlockSpec((None, a, b_), lambda j, te, *_: (te[j], 0, 0))
    hbm = pl.BlockSpec(memory_space=pl.ANY)
    tail_rows = ne * SEG_ALIGN
    return pl.pallas_call(
        functools.partial(_expert_ffn_body, ntp),
        grid_spec=pltpu.PrefetchScalarGridSpec(
            num_scalar_prefetch=n_meta,
            grid=(n_tiles,),
            in_specs=[hbm, hbm, wspec(d, de), wspec(1, de), wspec(d, de), wspec(1, de), wspec(de, d), wspec(1, d)],
            out_specs=[hbm, hbm],
            scratch_shapes=[
                pltpu.VMEM((2, tm, d), _F32), pltpu.VMEM((2, tm, d), _F32),
                pltpu.VMEM((tail_rows, d), _F32),
                pltpu.VMEM((d, de), _BF), pltpu.VMEM((d, de), _BF), pltpu.VMEM((de, d), _BF),
                pltpu.SemaphoreType.DMA((2,)), pltpu.SemaphoreType.DMA((2,)), pltpu.SemaphoreType.DMA(()),
            ],
        ),
        out_shape=[jax.ShapeDtypeStruct(xs_p.shape, _F32), jax.ShapeDtypeStruct(xs_s.shape, _F32)],
        compiler_params=pltpu.CompilerParams(dimension_semantics=("arbitrary",),
                                             vmem_limit_bytes=VMEM_LIMIT),
        name="expert_ffn",
    )(*meta, xs_p, xs_s, wg, bg, wu, bu, wd, bd)


def _combine_body(ys_ref, h_ref, pos_ref, prob_ref, gfin_ref, y_ref):
    tt, d = h_ref.shape
    slr = ys_ref.shape[0]
    ciota = lax.broadcasted_iota(jnp.int32, (tt, slr), 1).astype(_F32)
    w = jnp.zeros((tt, slr), _F32)
    for k in range(TOP_K):
        w = jnp.where(ciota == pos_ref[:, k:k + 1], prob_ref[:, k:k + 1], w)
    out = h_ref[...] + _dot(w.astype(_BF), ys_ref[...].astype(_BF))
    y_ref[...] = _rms(out, gfin_ref[...])


def _combine(ys, h, pos_t, prob_t, g_final, tt, slr):
    n, d = h.shape
    return pl.pallas_call(
        _combine_body,
        grid=(n // tt,),
        in_specs=[pl.BlockSpec((slr, d), lambda i: (i, 0)),
                  pl.BlockSpec((tt, d), lambda i: (i, 0)),
                  pl.BlockSpec((tt, TOP_K), lambda i: (i, 0)),
                  pl.BlockSpec((tt, TOP_K), lambda i: (i, 0)),
                  pl.BlockSpec((1, d), lambda i: (0, 0))],
        out_specs=pl.BlockSpec((tt, d), lambda i: (i, 0)),
        out_shape=jax.ShapeDtypeStruct((n, d), _F32),
        compiler_params=pltpu.CompilerParams(dimension_semantics=("arbitrary",),
                                             vmem_limit_bytes=VMEM_LIMIT),
        name="combine",
    )(ys, h, pos_t, prob_t, g_final)


def _tile_sizes(seq, nb, steps):
    tt = min(256, seq)
    return tt, tt // steps, 3 * ROW_QUANTUM


def _row_tile_tables(counts, tm, n_tiles, ntp, slr):
    nt_all, ne = counts.shape
    cp = (counts + (SEG_ALIGN - 1)) // SEG_ALIGN * SEG_ALIGN
    lo = jnp.cumsum(cp, axis=1) - cp
    base = jnp.cumsum(cp, axis=0) - cp
    used = jnp.sum(cp, axis=1)
    total = jnp.sum(cp, axis=0)
    tiles_e = (total + tm - 1) // tm
    tile_end = jnp.cumsum(tiles_e)
    first_tile = tile_end - tiles_e
    ids = jnp.arange(n_tiles, dtype=jnp.int32)
    te = jnp.minimum(jnp.sum((tile_end[None, :] <= ids[:, None]).astype(jnp.int32), axis=1), ne - 1)
    onehot = te[:, None] == jnp.arange(ne, dtype=jnp.int32)[None, :]
    pick = lambda v: jnp.sum(jnp.where(onehot, v[None, :], 0), axis=1)
    pick_row = lambda m: jnp.sum(jnp.where(onehot[:, :, None], m.T[None, :, :], 0), axis=1)
    r0 = (ids - pick(first_tile)) * tm
    valid = jnp.clip(pick(total) - r0, 0, tm)
    seg_start = pick_row(base)
    seg_end = seg_start + pick_row(cp)
    r1 = r0 + valid
    ilo = jnp.sum((seg_end <= r0[:, None]).astype(jnp.int32), axis=1)
    ihi = jnp.sum((seg_start < r1[:, None]).astype(jnp.int32), axis=1)
    first = jnp.maximum(seg_start, r0[:, None])
    p_n = jnp.maximum(jnp.minimum(seg_end, r1[:, None]) - first, 0)
    tile_row0 = jnp.where(jnp.arange(nt_all) < ntp, jnp.arange(nt_all), jnp.arange(nt_all) - ntp) * slr
    p_src = tile_row0[None, :] + pick_row(lo) + (first - seg_start)
    p_off = first - r0[:, None]
    tails = jnp.stack([tile_row0 + used, slr - used], axis=1)
    i32 = lambda a: a.astype(jnp.int32).reshape(-1)
    return [i32(te), i32(tile_end[-1:]), i32(valid), i32(ilo), i32(ihi), i32(p_src), i32(p_off), i32(p_n),
            i32(tails)]


def kernel(x_prompt, x_sample, state_conv_a, state_h, state_conf, g_mix, w_in, w_conv_a, b_conv_a,
           w_rg_a, b_rg_a, w_rg_x, b_rg_x, rg_lambda, w_conf_dw, b_conf_dw, g_conf_ln, b_conf_ln,
           w_conf_pw, b_conf_pw, w_out, g_ffn, w_router, b_router, w_gate, b_gate, w_up, b_up,
           w_down, b_down, g_final):
    depth = g_mix.shape[0]
    assert depth == 1
    bsz, seq, d = x_prompt.shape
    nb, steps, _ = x_sample.shape
    ne = w_router.shape[-1]
    kw = w_conv_a.shape[1]
    kc = w_conf_dw.shape[1]
    n_p = bsz * seq
    n_s = nb * steps
    tt, cb, tm = _tile_sizes(seq, nb, steps)
    assert seq % tt == 0 and nb % cb == 0 and cb * steps == tt
    slr = _sorted_rows(tt, ne)
    ntp, nts = n_p // tt, n_s // tt

    row = lambda a: a[0].reshape(1, -1)
    wts = dict(
        gmix=row(g_mix), win=w_in[0].astype(_BF), wca=w_conv_a[0], bca=row(b_conv_a),
        wra=w_rg_a[0].astype(_BF), bra=row(b_rg_a), wrx=w_rg_x[0].astype(_BF), brx=row(b_rg_x),
        lam=row(rg_lambda), wdw=w_conf_dw[0], bdw=row(b_conf_dw), gln=row(g_conf_ln), bln=row(b_conf_ln),
        wpw=w_conf_pw[0].astype(_BF), bpw=row(b_conf_pw), wout=w_out[0].astype(_BF), gffn=row(g_ffn),
        wrt=w_router[0].T.astype(_BF), br=b_router[0].reshape(ne, 1),
    )

    (h_p, xs_p, pos_p, prob_p, cnt_p, pca, ph, pcf) = _prompt_mixer(x_prompt, wts, tt)
    to_cm = lambda a: jnp.swapaxes(a.reshape(nb // cb, cb, a.shape[1], d), 1, 2).reshape(-1, d)
    (h_s, xs_s, pos_s, prob_s, cnt_s, v_s, sh_new, xa_s) = _sample_mixer(
        to_cm(x_sample), to_cm(state_conv_a[0]), state_h[0], to_cm(state_conf[0]), wts, cb, steps)

    counts = jnp.concatenate([cnt_p[:, 0].reshape(ntp, ne), cnt_s[:, 0].reshape(nts, ne)], axis=0)
    n_tiles = ((ntp + nts) * slr) // tm + ne + 2
    meta = _row_tile_tables(counts.astype(jnp.int32), tm, n_tiles, ntp, slr)

    ys_p, ys_s = _expert_ffn(meta, xs_p, xs_s, w_gate[0], b_gate[0].reshape(ne, 1, -1), w_up[0],
                             b_up[0].reshape(ne, 1, -1), w_down[0], b_down[0].reshape(ne, 1, -1), tm, slr, n_tiles)
    gfin = g_final.reshape(1, d)
    y_p = _combine(ys_p, h_p.reshape(n_p, d), pos_p.T, prob_p.T, gfin, tt, slr)
    y_s = _combine(ys_s, h_s, pos_s.T, prob_s.T, gfin, tt, slr)

    nat = lambda a: jnp.swapaxes(a.reshape(nb // cb, steps, cb, d), 1, 2).reshape(nb, steps, d)
    y_prompt = y_p.reshape(bsz, seq, d)
    y_sample = nat(y_s)
    p_conv_a = pca[:, CONV_A_HALO - (kw - 1):, :][None]
    p_h = ph.reshape(1, bsz, d)
    p_conf = pcf[:, CONF_HALO - (kc - 1):, :][None]
    s_conv_a = jnp.concatenate([state_conv_a[0], nat(xa_s)], axis=1)[:, -(kw - 1):, :][None]
    s_h = sh_new[None]
    s_conf = jnp.concatenate([state_conf[0], nat(v_s)], axis=1)[:, -(kc - 1):, :][None]
    return (y_prompt, y_sample, p_conv_a, p_h, p_conf, s_conv_a, s_h, s_conf)
```

```python
import functools

import jax
import jax.numpy as jnp
from jax import lax
from jax.experimental import pallas as pl
from jax.experimental.pallas import tpu as pltpu

EPS = 1e-6
RG_C = 8.0
SWIGLU_LIMIT = 7.0
SWIGLU_ALPHA = 1.702
TOP_K = 4
LANE = 128
SUBLANE = 8
CONV_A_HALO = 8
CONF_HALO = 32
TAP_GROUP = 8
ROW_QUANTUM = 256
SEG_ALIGN = SUBLANE
VMEM_LIMIT = 56 * 1024 * 1024

_BF = jnp.bfloat16
_F32 = jnp.float32


def _const_spec(shape):
    nd = len(shape)
    return pl.BlockSpec(shape, lambda *_: (0,) * nd, pipeline_mode=pl.Buffered(1))


def _sorted_rows(tt, ne):
    return TOP_K * tt + ne * SEG_ALIGN


def _rms(x, g):
    return x * lax.rsqrt(jnp.mean(x * x, axis=-1, keepdims=True) + EPS) * g


def _sigmoid(x):
    return jax.nn.sigmoid(x)


def _dot(a, b):
    return jnp.dot(a, b, preferred_element_type=_F32)


def _block_diag(xb, w_ref, b_ref, out_ref):
    nb, gb, _ = w_ref.shape
    for n in range(nb):
        cs = slice(n * gb, (n + 1) * gb)
        out_ref[:, cs] = _dot(xb[:, cs], w_ref[n]) + b_ref[:, cs]


def _rglru_coeffs(xc_ref, r_ref, i_ref, lam_ref, a_ref, u_ref):
    sp = jax.nn.softplus(-lam_ref[...])
    r = _sigmoid(r_ref[...])
    i = _sigmoid(i_ref[...])
    log_a = -RG_C * r * sp
    a = jnp.exp(log_a)
    a_ref[...] = a
    u_ref[...] = jnp.sqrt(-jnp.tanh(log_a) * (a * a + 1.0)) * (i * xc_ref[...])


def _layernorm_silu(vc, g, b):
    mu = jnp.mean(vc, axis=-1, keepdims=True)
    xc = vc - mu
    y = xc * lax.rsqrt(jnp.mean(xc * xc, axis=-1, keepdims=True) + EPS) * g + b
    return y * _sigmoid(y)


def _route_and_sort(xn_b, wrt_ref, br_ref, pos_ref, prob_ref, cnt_ref, xs_ref):
    ne = wrt_ref.shape[0]
    tt = xn_b.shape[0]
    slr = xs_ref.shape[0]
    logits = lax.dot_general(wrt_ref[...], xn_b, (((1,), (1,)), ((), ())),
                             preferred_element_type=_F32) + br_ref[...]
    eidx = lax.broadcasted_iota(jnp.int32, (ne, tt), 0).astype(_F32)
    work = logits
    vals, onehots = [], []
    for _ in range(TOP_K):
        m = jnp.max(work, axis=0, keepdims=True)
        pick = jnp.min(jnp.where(work == m, eidx, float(ne)), axis=0, keepdims=True)
        one = eidx == pick
        work = jnp.where(one, -jnp.inf, work)
        vals.append(m)
        onehots.append(one)
    exps = [jnp.exp(v - vals[0]) for v in vals]
    inv = 1.0 / (exps[0] + exps[1] + exps[2] + exps[3])
    sel = jnp.zeros((ne, tt), _F32)
    for one in onehots:
        sel = sel + one.astype(_F32)
    row = lax.broadcasted_iota(jnp.int32, (tt, tt), 0)
    col = lax.broadcasted_iota(jnp.int32, (tt, tt), 1)
    incl = _dot(sel.astype(_BF), (row <= col).astype(_BF))
    count = incl[:, tt - 1:tt]
    groups = jnp.floor((count + (SEG_ALIGN - 1)) * (1.0 / SEG_ALIGN))
    er = lax.broadcasted_iota(jnp.int32, (ne, ne), 0)
    ec = lax.broadcasted_iota(jnp.int32, (ne, ne), 1)
    lo = _dot((ec < er).astype(_BF), jnp.broadcast_to(groups, (ne, LANE)).astype(_BF))[:, 0:1] * SEG_ALIGN
    before = lo + (incl - sel)
    riota = lax.broadcasted_iota(jnp.int32, (slr, tt), 0).astype(_F32)
    perm = jnp.zeros((slr, tt), _F32)
    for k in range(TOP_K):
        pos = jnp.sum(jnp.where(onehots[k], before, 0.0), axis=0, keepdims=True)
        pos_ref[k:k + 1, :] = pos
        prob_ref[k:k + 1, :] = exps[k] * inv
        perm = jnp.where(riota == pos, 1.0, perm)
    cnt_ref[...] = jnp.broadcast_to(count, cnt_ref.shape)
    xs_ref[...] = _dot(perm.astype(_BF), xn_b)


def _tail(x, ya, yb, ga, gb, wout_ref, gffn_ref, h_ref):
    m = _sigmoid(ga) * ya + _sigmoid(gb) * yb
    h = x + _dot(m.astype(_BF), wout_ref[...])
    h_ref[...] = h
    return _rms(h, gffn_ref[...]).astype(_BF)


_W_NAMES = ["gmix", "win", "wca", "bca", "wra", "bra", "wrx", "brx", "lam", "wdw", "bdw", "gln", "bln",
            "wpw", "bpw", "wout", "gffn", "wrt", "br"]


def _causal_conv_tile(x, ext_ref, w_ref, b_ref, out_ref):
    tt, d = x.shape
    halo = ext_ref.shape[1] - tt
    kk = w_ref.shape[0]
    off = halo - (kk - 1)
    for c in range(d // LANE):
        cs = slice(c * LANE, (c + 1) * LANE)
        ext_ref[c, halo:halo + tt, :] = x[:, cs]
        bb = jnp.broadcast_to(b_ref[0:1, cs], (SUBLANE, LANE))
        for k0 in range(0, kk, TAP_GROUP):
            taps = range(k0, min(k0 + TAP_GROUP, kk))
            wb = {k: jnp.broadcast_to(w_ref[k:k + 1, cs], (SUBLANE, LANE)) for k in taps}
            for r0 in range(0, tt, SUBLANE):
                acc = bb if k0 == 0 else out_ref[r0:r0 + SUBLANE, cs]
                for k in taps:
                    acc = acc + wb[k] * ext_ref[c, r0 + off + k:r0 + off + k + SUBLANE, :]
                out_ref[r0:r0 + SUBLANE, cs] = acc
        ext_ref[c, 0:halo, :] = ext_ref[c, tt:tt + halo, :]


def _prompt_mixer_body(x_ref, gmix_ref, win_ref, wca_ref, bca_ref, wra_ref, bra_ref, wrx_ref, brx_ref,
                       lam_ref, wdw_ref, bdw_ref, gln_ref, bln_ref, wpw_ref, bpw_ref, wout_ref,
                       gffn_ref, wrt_ref, br_ref,
                       h_ref, xs_ref, pos_ref, prob_ref, cnt_ref, pca_ref, ph_ref, pcf_ref,
                       xa_ext, v_ext, s0, s1, s2, s3, s4, hcar):
    tt, d = x_ref.shape
    j = pl.program_id(1)

    @pl.when(j == 0)
    def _():
        hcar[...] = jnp.zeros_like(hcar)
        xa_ext[:, 0:CONV_A_HALO, :] = jnp.zeros((d // LANE, CONV_A_HALO, LANE), _F32)
        v_ext[:, 0:CONF_HALO, :] = jnp.zeros((d // LANE, CONF_HALO, LANE), _F32)

    x = x_ref[...]
    hn = _rms(x, gmix_ref[...]).astype(_BF)

    xa = _dot(hn, win_ref[:, 0:d])
    _causal_conv_tile(xa, xa_ext, wca_ref, bca_ref, s4)
    pca_ref[...] = xa[tt - CONV_A_HALO:tt, :]
    xcb = s4[...].astype(_BF)
    _block_diag(xcb, wra_ref, bra_ref, s0)
    _block_diag(xcb, wrx_ref, brx_ref, s1)
    _rglru_coeffs(s4, s0, s1, lam_ref, s2, s3)

    ri = lax.broadcasted_iota(jnp.int32, (SUBLANE, d), 0)

    h_last = hcar[...]
    for r0 in range(0, tt, SUBLANE):
        a = s2[r0:r0 + SUBLANE, :]
        u = s3[r0:r0 + SUBLANE, :]
        for s in (1, 2, 4):
            keep = ri >= s
            a_sh = jnp.where(keep, pltpu.roll(a, s, axis=0), 1.0)
            u_sh = jnp.where(keep, pltpu.roll(u, s, axis=0), 0.0)
            u = a * u_sh + u
            a = a * a_sh
        hh = a * h_last + u
        s0[r0:r0 + SUBLANE, :] = hh
        h_last = jnp.broadcast_to(hh[SUBLANE - 1:SUBLANE, :], (SUBLANE, d))
    hcar[...] = h_last
    ph_ref[...] = h_last[0:1, :]

    v = _dot(hn, win_ref[:, d:2 * d]) * _sigmoid(_dot(hn, win_ref[:, 2 * d:3 * d]))
    pcf_ref[...] = v[tt - CONF_HALO:tt, :]
    _causal_conv_tile(v, v_ext, wdw_ref, bdw_ref, s1)
    sl = _layernorm_silu(s1[...], gln_ref[...], bln_ref[...])
    yb = _dot(sl.astype(_BF), wpw_ref[...]) + bpw_ref[...]

    ga = _dot(hn, win_ref[:, 3 * d:4 * d])
    gb = _dot(hn, win_ref[:, 4 * d:5 * d])
    xn_b = _tail(x, s0[...], yb, ga, gb, wout_ref, gffn_ref, h_ref)
    _route_and_sort(xn_b, wrt_ref, br_ref, pos_ref, prob_ref, cnt_ref, xs_ref)


def _routing_outputs(n_tiles, tt, d, ne):
    slr = _sorted_rows(tt, ne)
    shapes = [
        jax.ShapeDtypeStruct((n_tiles * slr, d), _F32),
        jax.ShapeDtypeStruct((TOP_K, n_tiles * tt), _F32),
        jax.ShapeDtypeStruct((TOP_K, n_tiles * tt), _F32),
        jax.ShapeDtypeStruct((n_tiles * ne, LANE), _F32),
    ]
    blocks = [(slr, d), (TOP_K, tt), (TOP_K, tt), (ne, LANE)]
    return shapes, blocks


def _prompt_mixer(x, wts, tt):
    bsz, t, d = x.shape
    nt = t // tt
    ne = wts["wrt"].shape[0]
    w_args = [wts[n] for n in _W_NAMES]
    r_shapes, r_blocks = _routing_outputs(bsz * nt, tt, d, ne)
    tile_major = lambda blk: pl.BlockSpec(blk, (lambda b, j: (b * nt + j, 0)) if blk[0] != TOP_K
                                          else (lambda b, j: (0, b * nt + j)))
    out_shape = [jax.ShapeDtypeStruct((bsz, t, d), _F32)] + r_shapes + [
        jax.ShapeDtypeStruct((bsz, CONV_A_HALO, d), _F32),
        jax.ShapeDtypeStruct((bsz, 1, d), _F32),
        jax.ShapeDtypeStruct((bsz, CONF_HALO, d), _F32),
    ]
    out_specs = [pl.BlockSpec((None, tt, d), lambda b, j: (b, j, 0))] + [tile_major(blk) for blk in r_blocks] + [
        pl.BlockSpec((None, CONV_A_HALO, d), lambda b, j: (b, 0, 0)),
        pl.BlockSpec((None, 1, d), lambda b, j: (b, 0, 0)),
        pl.BlockSpec((None, CONF_HALO, d), lambda b, j: (b, 0, 0)),
    ]
    scratch = [
        pltpu.VMEM((d // LANE, tt + CONV_A_HALO, LANE), _F32),
        pltpu.VMEM((d // LANE, tt + CONF_HALO, LANE), _F32),
        pltpu.VMEM((tt, d), _F32), pltpu.VMEM((tt, d), _F32), pltpu.VMEM((tt, d), _F32),
        pltpu.VMEM((tt, d), _F32), pltpu.VMEM((tt, d), _F32),
        pltpu.VMEM((SUBLANE, d), _F32),
    ]
    return pl.pallas_call(
        _prompt_mixer_body,
        grid=(bsz, nt),
        in_specs=[pl.BlockSpec((None, tt, d), lambda b, j: (b, j, 0))] + [_const_spec(w.shape) for w in w_args],
        out_specs=out_specs,
        out_shape=out_shape,
        scratch_shapes=scratch,
        compiler_params=pltpu.CompilerParams(dimension_semantics=("arbitrary", "arbitrary"),
                                             vmem_limit_bytes=VMEM_LIMIT),
        name="prompt_mixer",
    )(x, *w_args)


def _sample_mixer_body(nb, x_ref, sca_ref, sh_ref, scf_ref,
                       gmix_ref, win_ref, wca_ref, bca_ref, wra_ref, bra_ref, wrx_ref, brx_ref,
                       lam_ref, wdw_ref, bdw_ref, gln_ref, bln_ref, wpw_ref, bpw_ref, wout_ref,
                       gffn_ref, wrt_ref, br_ref,
                       h_ref, xs_ref, pos_ref, prob_ref, cnt_ref, v_out_ref, sh_out_ref, xa_out_ref,
                       s0, s1, s2, s3, s4):
    n, d = x_ref.shape
    steps = n // nb
    x = x_ref[...]
    hn = _rms(x, gmix_ref[...]).astype(_BF)

    xa = _dot(hn, win_ref[:, 0:d])
    xa_out_ref[...] = xa
    kw = wca_ref.shape[0]

    def xcat_a(slab):
        if slab < kw - 1:
            return sca_ref[slab * nb:(slab + 1) * nb, :]
        s = slab - (kw - 1)
        return xa[s * nb:(s + 1) * nb, :]

    for t in range(steps):
        acc = bca_ref[...] + jnp.zeros((nb, d), _F32)
        for k in range(kw):
            acc = acc + wca_ref[k:k + 1, :] * xcat_a(t + k)
        s4[t * nb:(t + 1) * nb, :] = acc
    xcb = s4[...].astype(_BF)
    _block_diag(xcb, wra_ref, bra_ref, s0)
    _block_diag(xcb, wrx_ref, brx_ref, s1)
    _rglru_coeffs(s4, s0, s1, lam_ref, s2, s3)
    h = sh_ref[...]
    for t in range(steps):
        h = s2[t * nb:(t + 1) * nb, :] * h + s3[t * nb:(t + 1) * nb, :]
        s0[t * nb:(t + 1) * nb, :] = h
    sh_out_ref[...] = h

    v = _dot(hn, win_ref[:, d:2 * d]) * _sigmoid(_dot(hn, win_ref[:, 2 * d:3 * d]))
    v_out_ref[...] = v
    kc = wdw_ref.shape[0]

    def xcat_v(slab):
        if slab < kc - 1:
            return scf_ref[slab * nb:(slab + 1) * nb, :]
        s = slab - (kc - 1)
        return v[s * nb:(s + 1) * nb, :]

    for t in range(steps):
        acc = bdw_ref[...] + jnp.zeros((nb, d), _F32)
        for k in range(kc):
            acc = acc + wdw_ref[k:k + 1, :] * xcat_v(t + k)
        s1[t * nb:(t + 1) * nb, :] = acc
    sl = _layernorm_silu(s1[...], gln_ref[...], bln_ref[...])
    yb = _dot(sl.astype(_BF), wpw_ref[...]) + bpw_ref[...]

    ga = _dot(hn, win_ref[:, 3 * d:4 * d])
    gb = _dot(hn, win_ref[:, 4 * d:5 * d])
    xn_b = _tail(x, s0[...], yb, ga, gb, wout_ref, gffn_ref, h_ref)
    _route_and_sort(xn_b, wrt_ref, br_ref, pos_ref, prob_ref, cnt_ref, xs_ref)


def _sample_mixer(x_cm, sca_cm, sh, scf_cm, wts, cb, steps):
    n, d = x_cm.shape
    rows = cb * steps
    n_chunks = n // rows
    ne = wts["wrt"].shape[0]
    kw = wts["wca"].shape[0]
    kc = wts["wdw"].shape[0]
    w_args = [wts[n_] for n_ in _W_NAMES]
    chunk = lambda r: pl.BlockSpec((r, d), lambda i: (i, 0))
    r_shapes, r_blocks = _routing_outputs(n_chunks, rows, d, ne)
    tile_major = lambda blk: pl.BlockSpec(blk, (lambda i: (i, 0)) if blk[0] != TOP_K else (lambda i: (0, i)))
    conf_state = pl.BlockSpec(((kc - 1) * cb, d), lambda i: (i, 0), pipeline_mode=pl.Buffered(1))
    in_specs = [chunk(rows), chunk((kw - 1) * cb), chunk(cb), conf_state] + \
        [_const_spec(w.shape) for w in w_args]
    out_shape = [jax.ShapeDtypeStruct((n, d), _F32)] + r_shapes + [
        jax.ShapeDtypeStruct((n, d), _F32),
        jax.ShapeDtypeStruct(sh.shape, _F32),
        jax.ShapeDtypeStruct((n, d), _F32),
    ]
    out_specs = [chunk(rows)] + [tile_major(blk) for blk in r_blocks] + [chunk(rows), chunk(cb), chunk(rows)]
    scratch = [pltpu.VMEM((rows, d), _F32)] * 5
    return pl.pallas_call(
        functools.partial(_sample_mixer_body, cb),
        grid=(n_chunks,),
        in_specs=in_specs,
        out_specs=out_specs,
        out_shape=out_shape,
        scratch_shapes=scratch,
        compiler_params=pltpu.CompilerParams(dimension_semantics=("arbitrary",),
                                             vmem_limit_bytes=VMEM_LIMIT),
        name="sample_mixer",
    )(x_cm, sca_cm, sh, scf_cm, *w_args)


def _expert_ffn_body(ntp,
                     te_ref, nu_ref, valid_ref, ilo_ref, ihi_ref, psrc_ref, poff_ref, pn_ref, tail_ref,
                     xsp_ref, xss_ref, wg_ref, bg_ref, wu_ref, bu_ref, wd_ref, bd_ref,
                     ysp_ref, yss_ref,
                     lhs, obuf, zbuf, wg_b, wu_b, wd_b, gsem, ssem, zsem):
    j = pl.program_id(0)
    nu = nu_ref[0]
    nt_all = tail_ref.shape[0] // 2
    tm = lhs.shape[1]

    def for_pieces(step, fn):
        row0 = step * nt_all

        def run(group, first, last):
            def body(i, c):
                n = pn_ref[row0 + i]

                @pl.when(n > 0)
                def _():
                    fn(group, pl.multiple_of(psrc_ref[row0 + i], SEG_ALIGN),
                       pl.multiple_of(poff_ref[row0 + i], SEG_ALIGN), pl.multiple_of(n, SEG_ALIGN))

                return c

            lax.fori_loop(first, last, body, 0)

        run(0, ilo_ref[step], jnp.minimum(ihi_ref[step], ntp))
        run(1, jnp.maximum(ilo_ref[step], ntp), ihi_ref[step])

    def gather_start(step, slot):
        def fn(group, row, off, n):
            src = (xsp_ref, xss_ref)[group]
            pltpu.make_async_copy(src.at[pl.ds(row, n), :], lhs.at[slot, pl.ds(off, n), :], gsem.at[slot]).start()
        for_pieces(step, fn)

    def scatter_start(step, slot):
        def fn(group, row, off, n):
            dst = (ysp_ref, yss_ref)[group]
            pltpu.make_async_copy(obuf.at[slot, pl.ds(off, n), :], dst.at[pl.ds(row, n), :], ssem.at[slot]).start()
        for_pieces(step, fn)

    def gather_wait(step, slot):
        n = pl.multiple_of(valid_ref[step], SEG_ALIGN)
        pltpu.make_async_copy(xsp_ref.at[pl.ds(0, n), :], lhs.at[slot, pl.ds(0, n), :], gsem.at[slot]).wait()

    def scatter_wait(step, slot):
        n = pl.multiple_of(valid_ref[step], SEG_ALIGN)
        pltpu.make_async_copy(obuf.at[slot, pl.ds(0, n), :], ysp_ref.at[pl.ds(0, n), :], ssem.at[slot]).wait()

    def zero_tails(do_wait):
        def run(group, first, last):
            def one(i, c):
                row = pl.multiple_of(tail_ref[2 * i], SEG_ALIGN)
                n = pl.multiple_of(tail_ref[2 * i + 1], SEG_ALIGN)

                @pl.when(n > 0)
                def _():
                    dst = (ysp_ref, yss_ref)[group]
                    cp = pltpu.make_async_copy(zbuf.at[pl.ds(0, n), :], dst.at[pl.ds(row, n), :], zsem)
                    cp.wait() if do_wait else cp.start()

                return c

            lax.fori_loop(first, last, one, 0)

        run(0, 0, ntp)
        run(1, ntp, nt_all)

    @pl.when(j == 0)
    def _():
        lhs[...] = jnp.zeros_like(lhs)
        zbuf[...] = jnp.zeros_like(zbuf)
        zero_tails(False)
        zero_tails(True)
        gather_start(0, 0)

    @pl.when(j + 1 < nu)
    def _():
        gather_start(j + 1, (j + 1) % 2)

    slot = j % 2

    @pl.when((j >= 2) & (j - 2 < nu))
    def _():
        scatter_wait(j - 2, slot)

    def compute(rows):
        x = lhs[slot, 0:rows, :].astype(_BF)
        g = jnp.minimum(_dot(x, wg_b[...]) + bg_ref[...], SWIGLU_LIMIT)
        u = jnp.clip(_dot(x, wu_b[...]) + bu_ref[...], -SWIGLU_LIMIT, SWIGLU_LIMIT)
        hdn = (u + 1.0) * (g * _sigmoid(SWIGLU_ALPHA * g))
        obuf[slot, 0:rows, :] = _dot(hdn.astype(_BF), wd_b[...]) + bd_ref[...]

    @pl.when(j < nu)
    def _():
        gather_wait(j, slot)
        prev = te_ref[jnp.maximum(j - 1, 0)]

        @pl.when((j == 0) | (te_ref[j] != prev))
        def _():
            wg_b[...] = wg_ref[...].astype(_BF)
            wu_b[...] = wu_ref[...].astype(_BF)
            wd_b[...] = wd_ref[...].astype(_BF)

        for rows in range(ROW_QUANTUM, tm + 1, ROW_QUANTUM):
            @pl.when((valid_ref[j] > rows - ROW_QUANTUM) & (valid_ref[j] <= rows))
            def _(rows=rows):
                compute(rows)

        scatter_start(j, slot)


def _expert_ffn(meta, xs_p, xs_s, wg, bg, wu, bu, wd, bd, tm, slr, n_tiles):
    ne, d, de = wg.shape
    ntp = xs_p.shape[0] // slr
    n_meta = len(meta)
    wspec = lambda a, b_: pl.BlockSpec((None, a, b_), lambda j, te, *_: (te[j], 0, 0))
    hbm = pl.BlockSpec(memory_space=pl.ANY)
    tail_rows = ne * SEG_ALIGN
    return pl.pallas_call(
        functools.partial(_expert_ffn_body, ntp),
        grid_spec=pltpu.PrefetchScalarGridSpec(
            num_scalar_prefetch=n_meta,
            grid=(n_tiles,),
            in_specs=[hbm, hbm, wspec(d, de), wspec(1, de), wspec(d, de), wspec(1, de), wspec(de, d), wspec(1, d)],
            out_specs=[hbm, hbm],
            scratch_shapes=[
                pltpu.VMEM((2, tm, d), _F32), pltpu.VMEM((2, tm, d), _F32),
                pltpu.VMEM((tail_rows, d), _F32),
                pltpu.VMEM((d, de), _BF), pltpu.VMEM((d, de), _BF), pltpu.VMEM((de, d), _BF),
                pltpu.SemaphoreType.DMA((2,)), pltpu.SemaphoreType.DMA((2,)), pltpu.SemaphoreType.DMA(()),
            ],
        ),
        out_shape=[jax.ShapeDtypeStruct(xs_p.shape, _F32), jax.ShapeDtypeStruct(xs_s.shape, _F32)],
        compiler_params=pltpu.CompilerParams(dimension_semantics=("arbitrary",),
                                             vmem_limit_bytes=VMEM_LIMIT),
        name="expert_ffn",
    )(*meta, xs_p, xs_s, wg, bg, wu, bu, wd, bd)


def _combine_body(ys_ref, h_ref, pos_ref, prob_ref, gfin_ref, y_ref):
    tt, d = h_ref.shape
    slr = ys_ref.shape[0]
    ciota = lax.broadcasted_iota(jnp.int32, (tt, slr), 1).astype(_F32)
    w = jnp.zeros((tt, slr), _F32)
    for k in range(TOP_K):
        w = jnp.where(ciota == pos_ref[:, k:k + 1], prob_ref[:, k:k + 1], w)
    out = h_ref[...] + _dot(w.astype(_BF), ys_ref[...].astype(_BF))
    y_ref[...] = _rms(out, gfin_ref[...])


def _combine(ys, h, pos_t, prob_t, g_final, tt, slr):
    n, d = h.shape
    return pl.pallas_call(
        _combine_body,
        grid=(n // tt,),
        in_specs=[pl.BlockSpec((slr, d), lambda i: (i, 0)),
                  pl.BlockSpec((tt, d), lambda i: (i, 0)),
                  pl.BlockSpec((tt, TOP_K), lambda i: (i, 0)),
                  pl.BlockSpec((tt, TOP_K), lambda i: (i, 0)),
                  pl.BlockSpec((1, d), lambda i: (0, 0))],
        out_specs=pl.BlockSpec((tt, d), lambda i: (i, 0)),
        out_shape=jax.ShapeDtypeStruct((n, d), _F32),
        compiler_params=pltpu.CompilerParams(dimension_semantics=("arbitrary",),
                                             vmem_limit_bytes=VMEM_LIMIT),
        name="combine",
    )(ys, h, pos_t, prob_t, g_final)


def _tile_sizes(seq, nb, steps):
    tt = min(256, seq)
    return tt, tt // steps, 4 * ROW_QUANTUM


def _row_tile_tables(counts, tm, n_tiles, ntp, slr):
    nt_all, ne = counts.shape
    cp = (counts + (SEG_ALIGN - 1)) // SEG_ALIGN * SEG_ALIGN
    lo = jnp.cumsum(cp, axis=1) - cp
    base = jnp.cumsum(cp, axis=0) - cp
    used = jnp.sum(cp, axis=1)
    total = jnp.sum(cp, axis=0)
    tiles_e = (total + tm - 1) // tm
    tile_end = jnp.cumsum(tiles_e)
    first_tile = tile_end - tiles_e
    ids = jnp.arange(n_tiles, dtype=jnp.int32)
    te = jnp.minimum(jnp.sum((tile_end[None, :] <= ids[:, None]).astype(jnp.int32), axis=1), ne - 1)
    onehot = te[:, None] == jnp.arange(ne, dtype=jnp.int32)[None, :]
    pick = lambda v: jnp.sum(jnp.where(onehot, v[None, :], 0), axis=1)
    pick_row = lambda m: jnp.sum(jnp.where(onehot[:, :, None], m.T[None, :, :], 0), axis=1)
    r0 = (ids - pick(first_tile)) * tm
    valid = jnp.clip(pick(total) - r0, 0, tm)
    seg_start = pick_row(base)
    seg_end = seg_start + pick_row(cp)
    r1 = r0 + valid
    ilo = jnp.sum((seg_end <= r0[:, None]).astype(jnp.int32), axis=1)
    ihi = jnp.sum((seg_start < r1[:, None]).astype(jnp.int32), axis=1)
    first = jnp.maximum(seg_start, r0[:, None])
    p_n = jnp.maximum(jnp.minimum(seg_end, r1[:, None]) - first, 0)
    tile_row0 = jnp.where(jnp.arange(nt_all) < ntp, jnp.arange(nt_all), jnp.arange(nt_all) - ntp) * slr
    p_src = tile_row0[None, :] + pick_row(lo) + (first - seg_start)
    p_off = first - r0[:, None]
    tails = jnp.stack([tile_row0 + used, slr - used], axis=1)
    i32 = lambda a: a.astype(jnp.int32).reshape(-1)
    return [i32(te), i32(tile_end[-1:]), i32(valid), i32(ilo), i32(ihi), i32(p_src), i32(p_off), i32(p_n),
            i32(tails)]


def kernel(x_prompt, x_sample, state_conv_a, state_h, state_conf, g_mix, w_in, w_conv_a, b_conv_a,
           w_rg_a, b_rg_a, w_rg_x, b_rg_x, rg_lambda, w_conf_dw, b_conf_dw, g_conf_ln, b_conf_ln,
           w_conf_pw, b_conf_pw, w_out, g_ffn, w_router, b_router, w_gate, b_gate, w_up, b_up,
           w_down, b_down, g_final):
    depth = g_mix.shape[0]
    assert depth == 1
    bsz, seq, d = x_prompt.shape
    nb, steps, _ = x_sample.shape
    ne = w_router.shape[-1]
    kw = w_conv_a.shape[1]
    kc = w_conf_dw.shape[1]
    n_p = bsz * seq
    n_s = nb * steps
    tt, cb, tm = _tile_sizes(seq, nb, steps)
    assert seq % tt == 0 and nb % cb == 0 and cb * steps == tt
    slr = _sorted_rows(tt, ne)
    ntp, nts = n_p // tt, n_s // tt

    row = lambda a: a[0].reshape(1, -1)
    wts = dict(
        gmix=row(g_mix), win=w_in[0].astype(_BF), wca=w_conv_a[0], bca=row(b_conv_a),
        wra=w_rg_a[0].astype(_BF), bra=row(b_rg_a), wrx=w_rg_x[0].astype(_BF), brx=row(b_rg_x),
        lam=row(rg_lambda), wdw=w_conf_dw[0], bdw=row(b_conf_dw), gln=row(g_conf_ln), bln=row(b_conf_ln),
        wpw=w_conf_pw[0].astype(_BF), bpw=row(b_conf_pw), wout=w_out[0].astype(_BF), gffn=row(g_ffn),
        wrt=w_router[0].T.astype(_BF), br=b_router[0].reshape(ne, 1),
    )

    (h_p, xs_p, pos_p, prob_p, cnt_p, pca, ph, pcf) = _prompt_mixer(x_prompt, wts, tt)
    to_cm = lambda a: jnp.swapaxes(a.reshape(nb // cb, cb, a.shape[1], d), 1, 2).reshape(-1, d)
    (h_s, xs_s, pos_s, prob_s, cnt_s, v_s, sh_new, xa_s) = _sample_mixer(
        to_cm(x_sample), to_cm(state_conv_a[0]), state_h[0], to_cm(state_conf[0]), wts, cb, steps)

    counts = jnp.concatenate([cnt_p[:, 0].reshape(ntp, ne), cnt_s[:, 0].reshape(nts, ne)], axis=0)
    n_tiles = ((ntp + nts) * slr) // tm + ne + 2
    meta = _row_tile_tables(counts.astype(jnp.int32), tm, n_tiles, ntp, slr)

    ys_p, ys_s = _expert_ffn(meta, xs_p, xs_s, w_gate[0], b_gate[0].reshape(ne, 1, -1), w_up[0],
                             b_up[0].reshape(ne, 1, -1), w_down[0], b_down[0].reshape(ne, 1, -1), tm, slr, n_tiles)
    gfin = g_final.reshape(1, d)
    y_p = _combine(ys_p, h_p.reshape(n_p, d), pos_p.T, prob_p.T, gfin, tt, slr)
    y_s = _combine(ys_s, h_s, pos_s.T, prob_s.T, gfin, tt, slr)

    nat = lambda a: jnp.swapaxes(a.reshape(nb // cb, steps, cb, d), 1, 2).reshape(nb, steps, d)
    y_prompt = y_p.reshape(bsz, seq, d)
    y_sample = nat(y_s)
    p_conv_a = pca[:, CONV_A_HALO - (kw - 1):, :][None]
    p_h = ph.reshape(1, bsz, d)
    p_conf = pcf[:, CONF_HALO - (kc - 1):, :][None]
    s_conv_a = jnp.concatenate([state_conv_a[0], nat(xa_s)], axis=1)[:, -(kw - 1):, :][None]
    s_h = sh_new[None]
    s_conf = jnp.concatenate([state_conf[0], nat(v_s)], axis=1)[:, -(kc - 1):, :][None]
    return (y_prompt, y_sample, p_conv_a, p_h, p_conf, s_conv_a, s_h, s_conf)
```

```python
import functools

import jax
import jax.numpy as jnp
from jax import lax
from jax.experimental import pallas as pl
from jax.experimental.pallas import tpu as pltpu

EPS = 1e-6
RG_C = 8.0
SWIGLU_LIMIT = 7.0
SWIGLU_ALPHA = 1.702
TOP_K = 4
LANE = 128
SUBLANE = 8
CONV_A_HALO = 8
CONF_HALO = 32
TAP_GROUP = 8
COMBINE_TILES = 2
ROW_QUANTUM = 256
SEG_ALIGN = SUBLANE
VMEM_LIMIT = 56 * 1024 * 1024

_BF = jnp.bfloat16
_F32 = jnp.float32


def _const_spec(shape):
    nd = len(shape)
    return pl.BlockSpec(shape, lambda *_: (0,) * nd, pipeline_mode=pl.Buffered(1))


def _sorted_rows(tt, ne):
    return TOP_K * tt + ne * SEG_ALIGN


def _rms(x, g):
    return x * lax.rsqrt(jnp.mean(x * x, axis=-1, keepdims=True) + EPS) * g


def _sigmoid(x):
    return jax.nn.sigmoid(x)


def _dot(a, b):
    return jnp.dot(a, b, preferred_element_type=_F32)


def _block_diag(xb, w_ref, b_ref, out_ref):
    nb, gb, _ = w_ref.shape
    for n in range(nb):
        cs = slice(n * gb, (n + 1) * gb)
        out_ref[:, cs] = _dot(xb[:, cs], w_ref[n]) + b_ref[:, cs]


def _rglru_coeffs(xc_ref, r_ref, i_ref, lam_ref, a_ref, u_ref):
    sp = jax.nn.softplus(-lam_ref[...])
    r = _sigmoid(r_ref[...])
    i = _sigmoid(i_ref[...])
    log_a = -RG_C * r * sp
    a = jnp.exp(log_a)
    a_ref[...] = a
    u_ref[...] = jnp.sqrt(-jnp.tanh(log_a) * (a * a + 1.0)) * (i * xc_ref[...])


def _layernorm_silu(vc, g, b):
    mu = jnp.mean(vc, axis=-1, keepdims=True)
    xc = vc - mu
    y = xc * lax.rsqrt(jnp.mean(xc * xc, axis=-1, keepdims=True) + EPS) * g + b
    return y * _sigmoid(y)


def _route_and_sort(xn_b, wrt_ref, br_ref, pos_ref, prob_ref, cnt_ref, xs_ref):
    ne = wrt_ref.shape[0]
    tt = xn_b.shape[0]
    slr = xs_ref.shape[0]
    logits = lax.dot_general(wrt_ref[...], xn_b, (((1,), (1,)), ((), ())),
                             preferred_element_type=_F32) + br_ref[...]
    eidx = lax.broadcasted_iota(jnp.int32, (ne, tt), 0).astype(_F32)
    work = logits
    vals, onehots = [], []
    for _ in range(TOP_K):
        m = jnp.max(work, axis=0, keepdims=True)
        pick = jnp.min(jnp.where(work == m, eidx, float(ne)), axis=0, keepdims=True)
        one = eidx == pick
        work = jnp.where(one, -jnp.inf, work)
        vals.append(m)
        onehots.append(one)
    exps = [jnp.exp(v - vals[0]) for v in vals]
    inv = 1.0 / (exps[0] + exps[1] + exps[2] + exps[3])
    sel = jnp.zeros((ne, tt), _F32)
    for one in onehots:
        sel = sel + one.astype(_F32)
    row = lax.broadcasted_iota(jnp.int32, (tt, tt), 0)
    col = lax.broadcasted_iota(jnp.int32, (tt, tt), 1)
    incl = _dot(sel.astype(_BF), (row <= col).astype(_BF))
    count = incl[:, tt - 1:tt]
    groups = jnp.floor((count + (SEG_ALIGN - 1)) * (1.0 / SEG_ALIGN))
    er = lax.broadcasted_iota(jnp.int32, (ne, ne), 0)
    ec = lax.broadcasted_iota(jnp.int32, (ne, ne), 1)
    lo = _dot((ec < er).astype(_BF), jnp.broadcast_to(groups, (ne, LANE)).astype(_BF))[:, 0:1] * SEG_ALIGN
    before = lo + (incl - sel)
    riota = lax.broadcasted_iota(jnp.int32, (slr, tt), 0).astype(_F32)
    perm = jnp.zeros((slr, tt), _F32)
    for k in range(TOP_K):
        pos = jnp.sum(jnp.where(onehots[k], before, 0.0), axis=0, keepdims=True)
        pos_ref[k:k + 1, :] = pos
        prob_ref[k:k + 1, :] = exps[k] * inv
        perm = jnp.where(riota == pos, 1.0, perm)
    cnt_ref[...] = jnp.broadcast_to(count, cnt_ref.shape)
    xs_ref[...] = _dot(perm.astype(_BF), xn_b)


def _tail(x, ya, yb, ga, gb, wout_ref, gffn_ref, h_ref):
    m = _sigmoid(ga) * ya + _sigmoid(gb) * yb
    h = x + _dot(m.astype(_BF), wout_ref[...])
    h_ref[...] = h
    return _rms(h, gffn_ref[...]).astype(_BF)


_W_NAMES = ["gmix", "win", "wca", "bca", "wra", "bra", "wrx", "brx", "lam", "wdw", "bdw", "gln", "bln",
            "wpw", "bpw", "wout", "gffn", "wrt", "br"]


def _causal_conv_tile(x, ext_ref, w_ref, b_ref, out_ref):
    tt, d = x.shape
    halo = ext_ref.shape[1] - tt
    kk = w_ref.shape[0]
    off = halo - (kk - 1)
    for c in range(d // LANE):
        cs = slice(c * LANE, (c + 1) * LANE)
        ext_ref[c, halo:halo + tt, :] = x[:, cs]
        bb = jnp.broadcast_to(b_ref[0:1, cs], (SUBLANE, LANE))
        for k0 in range(0, kk, TAP_GROUP):
            taps = range(k0, min(k0 + TAP_GROUP, kk))
            wb = {k: jnp.broadcast_to(w_ref[k:k + 1, cs], (SUBLANE, LANE)) for k in taps}
            for r0 in range(0, tt, SUBLANE):
                acc = bb if k0 == 0 else out_ref[r0:r0 + SUBLANE, cs]
                for k in taps:
                    acc = acc + wb[k] * ext_ref[c, r0 + off + k:r0 + off + k + SUBLANE, :]
                out_ref[r0:r0 + SUBLANE, cs] = acc
        ext_ref[c, 0:halo, :] = ext_ref[c, tt:tt + halo, :]


def _prompt_mixer_body(x_ref, gmix_ref, win_ref, wca_ref, bca_ref, wra_ref, bra_ref, wrx_ref, brx_ref,
                       lam_ref, wdw_ref, bdw_ref, gln_ref, bln_ref, wpw_ref, bpw_ref, wout_ref,
                       gffn_ref, wrt_ref, br_ref,
                       h_ref, xs_ref, pos_ref, prob_ref, cnt_ref, pca_ref, ph_ref, pcf_ref,
                       xa_ext, v_ext, s0, s1, s2, s3, s4, hcar):
    tt, d = x_ref.shape
    j = pl.program_id(1)

    @pl.when(j == 0)
    def _():
        hcar[...] = jnp.zeros_like(hcar)
        xa_ext[:, 0:CONV_A_HALO, :] = jnp.zeros((d // LANE, CONV_A_HALO, LANE), _F32)
        v_ext[:, 0:CONF_HALO, :] = jnp.zeros((d // LANE, CONF_HALO, LANE), _F32)

    x = x_ref[...]
    hn = _rms(x, gmix_ref[...]).astype(_BF)

    xa = _dot(hn, win_ref[:, 0:d])
    _causal_conv_tile(xa, xa_ext, wca_ref, bca_ref, s4)
    pca_ref[...] = xa[tt - CONV_A_HALO:tt, :]
    xcb = s4[...].astype(_BF)
    _block_diag(xcb, wra_ref, bra_ref, s0)
    _block_diag(xcb, wrx_ref, brx_ref, s1)
    _rglru_coeffs(s4, s0, s1, lam_ref, s2, s3)

    ri = lax.broadcasted_iota(jnp.int32, (SUBLANE, d), 0)

    h_last = hcar[...]
    for r0 in range(0, tt, SUBLANE):
        a = s2[r0:r0 + SUBLANE, :]
        u = s3[r0:r0 + SUBLANE, :]
        for s in (1, 2, 4):
            keep = ri >= s
            a_sh = jnp.where(keep, pltpu.roll(a, s, axis=0), 1.0)
            u_sh = jnp.where(keep, pltpu.roll(u, s, axis=0), 0.0)
            u = a * u_sh + u
            a = a * a_sh
        hh = a * h_last + u
        s0[r0:r0 + SUBLANE, :] = hh
        h_last = jnp.broadcast_to(hh[SUBLANE - 1:SUBLANE, :], (SUBLANE, d))
    hcar[...] = h_last
    ph_ref[...] = h_last[0:1, :]

    v = _dot(hn, win_ref[:, d:2 * d]) * _sigmoid(_dot(hn, win_ref[:, 2 * d:3 * d]))
    pcf_ref[...] = v[tt - CONF_HALO:tt, :]
    _causal_conv_tile(v, v_ext, wdw_ref, bdw_ref, s1)
    sl = _layernorm_silu(s1[...], gln_ref[...], bln_ref[...])
    yb = _dot(sl.astype(_BF), wpw_ref[...]) + bpw_ref[...]

    ga = _dot(hn, win_ref[:, 3 * d:4 * d])
    gb = _dot(hn, win_ref[:, 4 * d:5 * d])
    xn_b = _tail(x, s0[...], yb, ga, gb, wout_ref, gffn_ref, h_ref)
    _route_and_sort(xn_b, wrt_ref, br_ref, pos_ref, prob_ref, cnt_ref, xs_ref)


def _routing_outputs(n_tiles, tt, d, ne):
    slr = _sorted_rows(tt, ne)
    shapes = [
        jax.ShapeDtypeStruct((n_tiles * slr, d), _F32),
        jax.ShapeDtypeStruct((TOP_K, n_tiles * tt), _F32),
        jax.ShapeDtypeStruct((TOP_K, n_tiles * tt), _F32),
        jax.ShapeDtypeStruct((n_tiles * ne, LANE), _F32),
    ]
    blocks = [(slr, d), (TOP_K, tt), (TOP_K, tt), (ne, LANE)]
    return shapes, blocks


def _prompt_mixer(x, wts, tt):
    bsz, t, d = x.shape
    nt = t // tt
    ne = wts["wrt"].shape[0]
    w_args = [wts[n] for n in _W_NAMES]
    r_shapes, r_blocks = _routing_outputs(bsz * nt, tt, d, ne)
    tile_major = lambda blk: pl.BlockSpec(blk, (lambda b, j: (b * nt + j, 0)) if blk[0] != TOP_K
                                          else (lambda b, j: (0, b * nt + j)))
    out_shape = [jax.ShapeDtypeStruct((bsz, t, d), _F32)] + r_shapes + [
        jax.ShapeDtypeStruct((bsz, CONV_A_HALO, d), _F32),
        jax.ShapeDtypeStruct((bsz, 1, d), _F32),
        jax.ShapeDtypeStruct((bsz, CONF_HALO, d), _F32),
    ]
    out_specs = [pl.BlockSpec((None, tt, d), lambda b, j: (b, j, 0))] + [tile_major(blk) for blk in r_blocks] + [
        pl.BlockSpec((None, CONV_A_HALO, d), lambda b, j: (b, 0, 0)),
        pl.BlockSpec((None, 1, d), lambda b, j: (b, 0, 0)),
        pl.BlockSpec((None, CONF_HALO, d), lambda b, j: (b, 0, 0)),
    ]
    scratch = [
        pltpu.VMEM((d // LANE, tt + CONV_A_HALO, LANE), _F32),
        pltpu.VMEM((d // LANE, tt + CONF_HALO, LANE), _F32),
        pltpu.VMEM((tt, d), _F32), pltpu.VMEM((tt, d), _F32), pltpu.VMEM((tt, d), _F32),
        pltpu.VMEM((tt, d), _F32), pltpu.VMEM((tt, d), _F32),
        pltpu.VMEM((SUBLANE, d), _F32),
    ]
    return pl.pallas_call(
        _prompt_mixer_body,
        grid=(bsz, nt),
        in_specs=[pl.BlockSpec((None, tt, d), lambda b, j: (b, j, 0))] + [_const_spec(w.shape) for w in w_args],
        out_specs=out_specs,
        out_shape=out_shape,
        scratch_shapes=scratch,
        compiler_params=pltpu.CompilerParams(dimension_semantics=("arbitrary", "arbitrary"),
                                             vmem_limit_bytes=VMEM_LIMIT),
        name="prompt_mixer",
    )(x, *w_args)


def _sample_mixer_body(nb, x_ref, sca_ref, sh_ref, scf_ref,
                       gmix_ref, win_ref, wca_ref, bca_ref, wra_ref, bra_ref, wrx_ref, brx_ref,
                       lam_ref, wdw_ref, bdw_ref, gln_ref, bln_ref, wpw_ref, bpw_ref, wout_ref,
                       gffn_ref, wrt_ref, br_ref,
                       h_ref, xs_ref, pos_ref, prob_ref, cnt_ref, v_out_ref, sh_out_ref, xa_out_ref,
                       s0, s1, s2, s3, s4):
    n, d = x_ref.shape
    steps = n // nb
    x = x_ref[...]
    hn = _rms(x, gmix_ref[...]).astype(_BF)

    xa = _dot(hn, win_ref[:, 0:d])
    xa_out_ref[...] = xa
    kw = wca_ref.shape[0]

    def xcat_a(slab):
        if slab < kw - 1:
            return sca_ref[slab * nb:(slab + 1) * nb, :]
        s = slab - (kw - 1)
        return xa[s * nb:(s + 1) * nb, :]

    for t in range(steps):
        acc = bca_ref[...] + jnp.zeros((nb, d), _F32)
        for k in range(kw):
            acc = acc + wca_ref[k:k + 1, :] * xcat_a(t + k)
        s4[t * nb:(t + 1) * nb, :] = acc
    xcb = s4[...].astype(_BF)
    _block_diag(xcb, wra_ref, bra_ref, s0)
    _block_diag(xcb, wrx_ref, brx_ref, s1)
    _rglru_coeffs(s4, s0, s1, lam_ref, s2, s3)
    h = sh_ref[...]
    for t in range(steps):
        h = s2[t * nb:(t + 1) * nb, :] * h + s3[t * nb:(t + 1) * nb, :]
        s0[t * nb:(t + 1) * nb, :] = h
    sh_out_ref[...] = h

    v = _dot(hn, win_ref[:, d:2 * d]) * _sigmoid(_dot(hn, win_ref[:, 2 * d:3 * d]))
    v_out_ref[...] = v
    kc = wdw_ref.shape[0]

    def xcat_v(slab):
        if slab < kc - 1:
            return scf_ref[slab * nb:(slab + 1) * nb, :]
        s = slab - (kc - 1)
        return v[s * nb:(s + 1) * nb, :]

    for t in range(steps):
        acc = bdw_ref[...] + jnp.zeros((nb, d), _F32)
        for k in range(kc):
            acc = acc + wdw_ref[k:k + 1, :] * xcat_v(t + k)
        s1[t * nb:(t + 1) * nb, :] = acc
    sl = _layernorm_silu(s1[...], gln_ref[...], bln_ref[...])
    yb = _dot(sl.astype(_BF), wpw_ref[...]) + bpw_ref[...]

    ga = _dot(hn, win_ref[:, 3 * d:4 * d])
    gb = _dot(hn, win_ref[:, 4 * d:5 * d])
    xn_b = _tail(x, s0[...], yb, ga, gb, wout_ref, gffn_ref, h_ref)
    _route_and_sort(xn_b, wrt_ref, br_ref, pos_ref, prob_ref, cnt_ref, xs_ref)


def _sample_mixer(x_cm, sca_cm, sh, scf_cm, wts, cb, steps):
    n, d = x_cm.shape
    rows = cb * steps
    n_chunks = n // rows
    ne = wts["wrt"].shape[0]
    kw = wts["wca"].shape[0]
    kc = wts["wdw"].shape[0]
    w_args = [wts[n_] for n_ in _W_NAMES]
    chunk = lambda r: pl.BlockSpec((r, d), lambda i: (i, 0))
    r_shapes, r_blocks = _routing_outputs(n_chunks, rows, d, ne)
    tile_major = lambda blk: pl.BlockSpec(blk, (lambda i: (i, 0)) if blk[0] != TOP_K else (lambda i: (0, i)))
    conf_state = pl.BlockSpec(((kc - 1) * cb, d), lambda i: (i, 0), pipeline_mode=pl.Buffered(1))
    in_specs = [chunk(rows), chunk((kw - 1) * cb), chunk(cb), conf_state] + \
        [_const_spec(w.shape) for w in w_args]
    out_shape = [jax.ShapeDtypeStruct((n, d), _F32)] + r_shapes + [
        jax.ShapeDtypeStruct((n, d), _F32),
        jax.ShapeDtypeStruct(sh.shape, _F32),
        jax.ShapeDtypeStruct((n, d), _F32),
    ]
    out_specs = [chunk(rows)] + [tile_major(blk) for blk in r_blocks] + [chunk(rows), chunk(cb), chunk(rows)]
    scratch = [pltpu.VMEM((rows, d), _F32)] * 5
    return pl.pallas_call(
        functools.partial(_sample_mixer_body, cb),
        grid=(n_chunks,),
        in_specs=in_specs,
        out_specs=out_specs,
        out_shape=out_shape,
        scratch_shapes=scratch,
        compiler_params=pltpu.CompilerParams(dimension_semantics=("arbitrary",),
                                             vmem_limit_bytes=VMEM_LIMIT),
        name="sample_mixer",
    )(x_cm, sca_cm, sh, scf_cm, *w_args)


def _expert_ffn_body(ntp,
                     te_ref, nu_ref, valid_ref, ilo_ref, ihi_ref, psrc_ref, poff_ref, pn_ref, tail_ref,
                     xsp_ref, xss_ref, wg_ref, bg_ref, wu_ref, bu_ref, wd_ref, bd_ref,
                     ysp_ref, yss_ref,
                     lhs, obuf, zbuf, wg_b, wu_b, wd_b, gsem, ssem, zsem):
    j = pl.program_id(0)
    nu = nu_ref[0]
    nt_all = tail_ref.shape[0] // 2
    tm = lhs.shape[1]

    def for_pieces(step, fn):
        row0 = step * nt_all

        def run(group, first, last):
            def body(i, c):
                n = pn_ref[row0 + i]

                @pl.when(n > 0)
                def _():
                    fn(group, pl.multiple_of(psrc_ref[row0 + i], SEG_ALIGN),
                       pl.multiple_of(poff_ref[row0 + i], SEG_ALIGN), pl.multiple_of(n, SEG_ALIGN))

                return c

            lax.fori_loop(first, last, body, 0)

        run(0, ilo_ref[step], jnp.minimum(ihi_ref[step], ntp))
        run(1, jnp.maximum(ilo_ref[step], ntp), ihi_ref[step])

    def gather_start(step, slot):
        def fn(group, row, off, n):
            src = (xsp_ref, xss_ref)[group]
            pltpu.make_async_copy(src.at[pl.ds(row, n), :], lhs.at[slot, pl.ds(off, n), :], gsem.at[slot]).start()
        for_pieces(step, fn)

    def scatter_start(step, slot):
        def fn(group, row, off, n):
            dst = (ysp_ref, yss_ref)[group]
            pltpu.make_async_copy(obuf.at[slot, pl.ds(off, n), :], dst.at[pl.ds(row, n), :], ssem.at[slot]).start()
        for_pieces(step, fn)

    def gather_wait(step, slot):
        n = pl.multiple_of(valid_ref[step], SEG_ALIGN)
        pltpu.make_async_copy(xsp_ref.at[pl.ds(0, n), :], lhs.at[slot, pl.ds(0, n), :], gsem.at[slot]).wait()

    def scatter_wait(step, slot):
        n = pl.multiple_of(valid_ref[step], SEG_ALIGN)
        pltpu.make_async_copy(obuf.at[slot, pl.ds(0, n), :], ysp_ref.at[pl.ds(0, n), :], ssem.at[slot]).wait()

    def zero_tails(do_wait):
        def run(group, first, last):
            def one(i, c):
                row = pl.multiple_of(tail_ref[2 * i], SEG_ALIGN)
                n = pl.multiple_of(tail_ref[2 * i + 1], SEG_ALIGN)

                @pl.when(n > 0)
                def _():
                    dst = (ysp_ref, yss_ref)[group]
                    cp = pltpu.make_async_copy(zbuf.at[pl.ds(0, n), :], dst.at[pl.ds(row, n), :], zsem)
                    cp.wait() if do_wait else cp.start()

                return c

            lax.fori_loop(first, last, one, 0)

        run(0, 0, ntp)
        run(1, ntp, nt_all)

    @pl.when(j == 0)
    def _():
        lhs[...] = jnp.zeros_like(lhs)
        zbuf[...] = jnp.zeros_like(zbuf)
        zero_tails(False)
        zero_tails(True)
        gather_start(0, 0)

    @pl.when(j + 1 < nu)
    def _():
        gather_start(j + 1, (j + 1) % 2)

    slot = j % 2

    @pl.when((j >= 2) & (j - 2 < nu))
    def _():
        scatter_wait(j - 2, slot)

    def compute(rows):
        x = lhs[slot, 0:rows, :].astype(_BF)
        g = jnp.minimum(_dot(x, wg_b[...]) + bg_ref[...], SWIGLU_LIMIT)
        u = jnp.clip(_dot(x, wu_b[...]) + bu_ref[...], -SWIGLU_LIMIT, SWIGLU_LIMIT)
        hdn = (u + 1.0) * (g * _sigmoid(SWIGLU_ALPHA * g))
        obuf[slot, 0:rows, :] = _dot(hdn.astype(_BF), wd_b[...]) + bd_ref[...]

    @pl.when(j < nu)
    def _():
        gather_wait(j, slot)
        prev = te_ref[jnp.maximum(j - 1, 0)]

        @pl.when((j == 0) | (te_ref[j] != prev))
        def _():
            wg_b[...] = wg_ref[...].astype(_BF)
            wu_b[...] = wu_ref[...].astype(_BF)
            wd_b[...] = wd_ref[...].astype(_BF)

        for rows in range(ROW_QUANTUM, tm + 1, ROW_QUANTUM):
            @pl.when((valid_ref[j] > rows - ROW_QUANTUM) & (valid_ref[j] <= rows))
            def _(rows=rows):
                compute(rows)

        scatter_start(j, slot)


def _expert_ffn(meta, xs_p, xs_s, wg, bg, wu, bu, wd, bd, tm, slr, n_tiles):
    ne, d, de = wg.shape
    ntp = xs_p.shape[0] // slr
    n_meta = len(meta)
    wspec = lambda a, b_: pl.BlockSpec((None, a, b_), lambda j, te, *_: (te[j], 0, 0))
    hbm = pl.BlockSpec(memory_space=pl.ANY)
    tail_rows = ne * SEG_ALIGN
    return pl.pallas_call(
        functools.partial(_expert_ffn_body, ntp),
        grid_spec=pltpu.PrefetchScalarGridSpec(
            num_scalar_prefetch=n_meta,
            grid=(n_tiles,),
            in_specs=[hbm, hbm, wspec(d, de), wspec(1, de), wspec(d, de), wspec(1, de), wspec(de, d), wspec(1, d)],
            out_specs=[hbm, hbm],
            scratch_shapes=[
                pltpu.VMEM((2, tm, d), _F32), pltpu.VMEM((2, tm, d), _F32),
                pltpu.VMEM((tail_rows, d), _F32),
                pltpu.VMEM((d, de), _BF), pltpu.VMEM((d, de), _BF), pltpu.VMEM((de, d), _BF),
                pltpu.SemaphoreType.DMA((2,)), pltpu.SemaphoreType.DMA((2,)), pltpu.SemaphoreType.DMA(()),
            ],
        ),
        out_shape=[jax.ShapeDtypeStruct(xs_p.shape, _F32), jax.ShapeDtypeStruct(xs_s.shape, _F32)],
        compiler_params=pltpu.CompilerParams(dimension_semantics=("arbitrary",),
                                             vmem_limit_bytes=VMEM_LIMIT),
        name="expert_ffn",
    )(*meta, xs_p, xs_s, wg, bg, wu, bu, wd, bd)


def _combine_body(tt, slr, ys_ref, h_ref, pos_ref, prob_ref, gfin_ref, y_ref):
    ciota = lax.broadcasted_iota(jnp.int32, (tt, slr), 1).astype(_F32)
    for q in range(h_ref.shape[0] // tt):
        rows = slice(q * tt, (q + 1) * tt)
        w = jnp.zeros((tt, slr), _F32)
        for k in range(TOP_K):
            w = jnp.where(ciota == pos_ref[rows, k:k + 1], prob_ref[rows, k:k + 1], w)
        ys = ys_ref[q * slr:(q + 1) * slr, :].astype(_BF)
        out = h_ref[rows, :] + _dot(w.astype(_BF), ys)
        y_ref[rows, :] = _rms(out, gfin_ref[...])


def _combine(ys, h, pos_t, prob_t, g_final, tt, slr):
    n, d = h.shape
    per_step = COMBINE_TILES if (n // tt) % COMBINE_TILES == 0 else 1
    slr, tt_step = slr * per_step, tt * per_step
    return pl.pallas_call(
        functools.partial(_combine_body, tt, slr // per_step),
        grid=(n // tt_step,),
        in_specs=[pl.BlockSpec((slr, d), lambda i: (i, 0)),
                  pl.BlockSpec((tt_step, d), lambda i: (i, 0)),
                  pl.BlockSpec((tt_step, TOP_K), lambda i: (i, 0)),
                  pl.BlockSpec((tt_step, TOP_K), lambda i: (i, 0)),
                  pl.BlockSpec((1, d), lambda i: (0, 0))],
        out_specs=pl.BlockSpec((tt_step, d), lambda i: (i, 0)),
        out_shape=jax.ShapeDtypeStruct((n, d), _F32),
        compiler_params=pltpu.CompilerParams(dimension_semantics=("arbitrary",),
                                             vmem_limit_bytes=VMEM_LIMIT),
        name="combine",
    )(ys, h, pos_t, prob_t, g_final)


def _tile_sizes(seq, nb, steps):
    tt = min(256, seq)
    return tt, tt // steps, 4 * ROW_QUANTUM


def _row_tile_tables(counts, tm, n_tiles, ntp, slr):
    nt_all, ne = counts.shape
    cp = (counts + (SEG_ALIGN - 1)) // SEG_ALIGN * SEG_ALIGN
    lo = jnp.cumsum(cp, axis=1) - cp
    base = jnp.cumsum(cp, axis=0) - cp
    used = jnp.sum(cp, axis=1)
    total = jnp.sum(cp, axis=0)
    tiles_e = (total + tm - 1) // tm
    tile_end = jnp.cumsum(tiles_e)
    first_tile = tile_end - tiles_e
    ids = jnp.arange(n_tiles, dtype=jnp.int32)
    te = jnp.minimum(jnp.sum((tile_end[None, :] <= ids[:, None]).astype(jnp.int32), axis=1), ne - 1)
    onehot = te[:, None] == jnp.arange(ne, dtype=jnp.int32)[None, :]
    pick = lambda v: jnp.sum(jnp.where(onehot, v[None, :], 0), axis=1)
    pick_row = lambda m: jnp.sum(jnp.where(onehot[:, :, None], m.T[None, :, :], 0), axis=1)
    r0 = (ids - pick(first_tile)) * tm
    valid = jnp.clip(pick(total) - r0, 0, tm)
    seg_start = pick_row(base)
    seg_end = seg_start + pick_row(cp)
    r1 = r0 + valid
    ilo = jnp.sum((seg_end <= r0[:, None]).astype(jnp.int32), axis=1)
    ihi = jnp.sum((seg_start < r1[:, None]).astype(jnp.int32), axis=1)
    first = jnp.maximum(seg_start, r0[:, None])
    p_n = jnp.maximum(jnp.minimum(seg_end, r1[:, None]) - first, 0)
    tile_row0 = jnp.where(jnp.arange(nt_all) < ntp, jnp.arange(nt_all), jnp.arange(nt_all) - ntp) * slr
    p_src = tile_row0[None, :] + pick_row(lo) + (first - seg_start)
    p_off = first - r0[:, None]
    tails = jnp.stack([tile_row0 + used, slr - used], axis=1)
    i32 = lambda a: a.astype(jnp.int32).reshape(-1)
    return [i32(te), i32(tile_end[-1:]), i32(valid), i32(ilo), i32(ihi), i32(p_src), i32(p_off), i32(p_n),
            i32(tails)]


def kernel(x_prompt, x_sample, state_conv_a, state_h, state_conf, g_mix, w_in, w_conv_a, b_conv_a,
           w_rg_a, b_rg_a, w_rg_x, b_rg_x, rg_lambda, w_conf_dw, b_conf_dw, g_conf_ln, b_conf_ln,
           w_conf_pw, b_conf_pw, w_out, g_ffn, w_router, b_router, w_gate, b_gate, w_up, b_up,
           w_down, b_down, g_final):
    depth = g_mix.shape[0]
    assert depth == 1
    bsz, seq, d = x_prompt.shape
    nb, steps, _ = x_sample.shape
    ne = w_router.shape[-1]
    kw = w_conv_a.shape[1]
    kc = w_conf_dw.shape[1]
    n_p = bsz * seq
    n_s = nb * steps
    tt, cb, tm = _tile_sizes(seq, nb, steps)
    assert seq % tt == 0 and nb % cb == 0 and cb * steps == tt
    slr = _sorted_rows(tt, ne)
    ntp, nts = n_p // tt, n_s // tt

    row = lambda a: a[0].reshape(1, -1)
    wts = dict(
        gmix=row(g_mix), win=w_in[0].astype(_BF), wca=w_conv_a[0], bca=row(b_conv_a),
        wra=w_rg_a[0].astype(_BF), bra=row(b_rg_a), wrx=w_rg_x[0].astype(_BF), brx=row(b_rg_x),
        lam=row(rg_lambda), wdw=w_conf_dw[0], bdw=row(b_conf_dw), gln=row(g_conf_ln), bln=row(b_conf_ln),
        wpw=w_conf_pw[0].astype(_BF), bpw=row(b_conf_pw), wout=w_out[0].astype(_BF), gffn=row(g_ffn),
        wrt=w_router[0].T.astype(_BF), br=b_router[0].reshape(ne, 1),
    )

    (h_p, xs_p, pos_p, prob_p, cnt_p, pca, ph, pcf) = _prompt_mixer(x_prompt, wts, tt)
    to_cm = lambda a: jnp.swapaxes(a.reshape(nb // cb, cb, a.shape[1], d), 1, 2).reshape(-1, d)
    (h_s, xs_s, pos_s, prob_s, cnt_s, v_s, sh_new, xa_s) = _sample_mixer(
        to_cm(x_sample), to_cm(state_conv_a[0]), state_h[0], to_cm(state_conf[0]), wts, cb, steps)

    counts = jnp.concatenate([cnt_p[:, 0].reshape(ntp, ne), cnt_s[:, 0].reshape(nts, ne)], axis=0)
    n_tiles = ((ntp + nts) * slr) // tm + ne + 2
    meta = _row_tile_tables(counts.astype(jnp.int32), tm, n_tiles, ntp, slr)

    ys_p, ys_s = _expert_ffn(meta, xs_p, xs_s, w_gate[0], b_gate[0].reshape(ne, 1, -1), w_up[0],
                             b_up[0].reshape(ne, 1, -1), w_down[0], b_down[0].reshape(ne, 1, -1), tm, slr, n_tiles)
    gfin = g_final.reshape(1, d)
    y_p = _combine(ys_p, h_p.reshape(n_p, d), pos_p.T, prob_p.T, gfin, tt, slr)
    y_s = _combine(ys_s, h_s, pos_s.T, prob_s.T, gfin, tt, slr)

    nat = lambda a: jnp.swapaxes(a.reshape(nb // cb, steps, cb, d), 1, 2).reshape(nb, steps, d)
    y_prompt = y_p.reshape(bsz, seq, d)
    y_sample = nat(y_s)
    p_conv_a = pca[:, CONV_A_HALO - (kw - 1):, :][None]
    p_h = ph.reshape(1, bsz, d)
    p_conf = pcf[:, CONF_HALO - (kc - 1):, :][None]
    s_conv_a = jnp.concatenate([state_conv_a[0], nat(xa_s)], axis=1)[:, -(kw - 1):, :][None]
    s_h = sh_new[None]
    s_conf = jnp.concatenate([state_conf[0], nat(v_s)], axis=1)[:, -(kc - 1):, :][None]
    return (y_prompt, y_sample, p_conv_a, p_h, p_conf, s_conv_a, s_h, s_conf)
```

```python
import functools

import jax
import jax.numpy as jnp
from jax import lax
from jax.experimental import pallas as pl
from jax.experimental.pallas import tpu as pltpu

EPS = 1e-6
RG_C = 8.0
SWIGLU_LIMIT = 7.0
SWIGLU_ALPHA = 1.702
TOP_K = 4
LANE = 128
SUBLANE = 8
CONV_A_HALO = 8
CONF_HALO = 32
TAP_GROUP = 8
COMBINE_TILES = 2
ROW_QUANTUM = 128
ROW_TILE = 1024
SEG_ALIGN = SUBLANE
VMEM_LIMIT = 56 * 1024 * 1024

_BF = jnp.bfloat16
_F32 = jnp.float32


def _const_spec(shape):
    nd = len(shape)
    return pl.BlockSpec(shape, lambda *_: (0,) * nd, pipeline_mode=pl.Buffered(1))


def _sorted_rows(tt, ne):
    return TOP_K * tt + ne * SEG_ALIGN


def _rms(x, g):
    return x * lax.rsqrt(jnp.mean(x * x, axis=-1, keepdims=True) + EPS) * g


def _sigmoid(x):
    return jax.nn.sigmoid(x)


def _dot(a, b):
    return jnp.dot(a, b, preferred_element_type=_F32)


def _block_diag(xb, w_ref, b_ref, out_ref):
    nb, gb, _ = w_ref.shape
    for n in range(nb):
        cs = slice(n * gb, (n + 1) * gb)
        out_ref[:, cs] = _dot(xb[:, cs], w_ref[n]) + b_ref[:, cs]


def _rglru_coeffs(xc_ref, r_ref, i_ref, lam_ref, a_ref, u_ref):
    sp = jax.nn.softplus(-lam_ref[...])
    r = _sigmoid(r_ref[...])
    i = _sigmoid(i_ref[...])
    log_a = -RG_C * r * sp
    a = jnp.exp(log_a)
    a_ref[...] = a
    u_ref[...] = jnp.sqrt(-jnp.tanh(log_a) * (a * a + 1.0)) * (i * xc_ref[...])


def _layernorm_silu(vc, g, b):
    mu = jnp.mean(vc, axis=-1, keepdims=True)
    xc = vc - mu
    y = xc * lax.rsqrt(jnp.mean(xc * xc, axis=-1, keepdims=True) + EPS) * g + b
    return y * _sigmoid(y)


def _route_and_sort(xn_b, wrt_ref, br_ref, pos_ref, prob_ref, cnt_ref, xs_ref):
    ne = wrt_ref.shape[0]
    tt = xn_b.shape[0]
    slr = xs_ref.shape[0]
    logits = lax.dot_general(wrt_ref[...], xn_b, (((1,), (1,)), ((), ())),
                             preferred_element_type=_F32) + br_ref[...]
    eidx = lax.broadcasted_iota(jnp.int32, (ne, tt), 0).astype(_F32)
    work = logits
    vals, onehots = [], []
    for _ in range(TOP_K):
        m = jnp.max(work, axis=0, keepdims=True)
        pick = jnp.min(jnp.where(work == m, eidx, float(ne)), axis=0, keepdims=True)
        one = eidx == pick
        work = jnp.where(one, -jnp.inf, work)
        vals.append(m)
        onehots.append(one)
    exps = [jnp.exp(v - vals[0]) for v in vals]
    inv = 1.0 / (exps[0] + exps[1] + exps[2] + exps[3])
    sel = jnp.zeros((ne, tt), _F32)
    for one in onehots:
        sel = sel + one.astype(_F32)
    row = lax.broadcasted_iota(jnp.int32, (tt, tt), 0)
    col = lax.broadcasted_iota(jnp.int32, (tt, tt), 1)
    incl = _dot(sel.astype(_BF), (row <= col).astype(_BF))
    count = incl[:, tt - 1:tt]
    groups = jnp.floor((count + (SEG_ALIGN - 1)) * (1.0 / SEG_ALIGN))
    er = lax.broadcasted_iota(jnp.int32, (ne, ne), 0)
    ec = lax.broadcasted_iota(jnp.int32, (ne, ne), 1)
    lo = _dot((ec < er).astype(_BF), jnp.broadcast_to(groups, (ne, LANE)).astype(_BF))[:, 0:1] * SEG_ALIGN
    before = lo + (incl - sel)
    riota = lax.broadcasted_iota(jnp.int32, (slr, tt), 0).astype(_F32)
    perm = jnp.zeros((slr, tt), _F32)
    for k in range(TOP_K):
        pos = jnp.sum(jnp.where(onehots[k], before, 0.0), axis=0, keepdims=True)
        pos_ref[k:k + 1, :] = pos
        prob_ref[k:k + 1, :] = exps[k] * inv
        perm = jnp.where(riota == pos, 1.0, perm)
    cnt_ref[...] = jnp.broadcast_to(count, cnt_ref.shape)
    xs_ref[...] = _dot(perm.astype(_BF), xn_b)


def _tail(x, ya, yb, ga, gb, wout_ref, gffn_ref, h_ref):
    m = _sigmoid(ga) * ya + _sigmoid(gb) * yb
    h = x + _dot(m.astype(_BF), wout_ref[...])
    h_ref[...] = h
    return _rms(h, gffn_ref[...]).astype(_BF)


_W_NAMES = ["gmix", "win", "wca", "bca", "wra", "bra", "wrx", "brx", "lam", "wdw", "bdw", "gln", "bln",
            "wpw", "bpw", "wout", "gffn", "wrt", "br"]


def _causal_conv_tile(x, ext_ref, w_ref, b_ref, out_ref):
    tt, d = x.shape
    halo = ext_ref.shape[1] - tt
    kk = w_ref.shape[0]
    off = halo - (kk - 1)
    for c in range(d // LANE):
        cs = slice(c * LANE, (c + 1) * LANE)
        ext_ref[c, halo:halo + tt, :] = x[:, cs]
        bb = jnp.broadcast_to(b_ref[0:1, cs], (SUBLANE, LANE))
        for k0 in range(0, kk, TAP_GROUP):
            taps = range(k0, min(k0 + TAP_GROUP, kk))
            wb = {k: jnp.broadcast_to(w_ref[k:k + 1, cs], (SUBLANE, LANE)) for k in taps}
            for r0 in range(0, tt, SUBLANE):
                acc = bb if k0 == 0 else out_ref[r0:r0 + SUBLANE, cs]
                for k in taps:
                    acc = acc + wb[k] * ext_ref[c, r0 + off + k:r0 + off + k + SUBLANE, :]
                out_ref[r0:r0 + SUBLANE, cs] = acc
        ext_ref[c, 0:halo, :] = ext_ref[c, tt:tt + halo, :]


def _prompt_mixer_body(x_ref, gmix_ref, win_ref, wca_ref, bca_ref, wra_ref, bra_ref, wrx_ref, brx_ref,
                       lam_ref, wdw_ref, bdw_ref, gln_ref, bln_ref, wpw_ref, bpw_ref, wout_ref,
                       gffn_ref, wrt_ref, br_ref,
                       h_ref, xs_ref, pos_ref, prob_ref, cnt_ref, pca_ref, ph_ref, pcf_ref,
                       xa_ext, v_ext, s0, s1, s2, s3, s4, hcar):
    tt, d = x_ref.shape
    j = pl.program_id(1)

    @pl.when(j == 0)
    def _():
        hcar[...] = jnp.zeros_like(hcar)
        xa_ext[:, 0:CONV_A_HALO, :] = jnp.zeros((d // LANE, CONV_A_HALO, LANE), _F32)
        v_ext[:, 0:CONF_HALO, :] = jnp.zeros((d // LANE, CONF_HALO, LANE), _F32)

    x = x_ref[...]
    hn = _rms(x, gmix_ref[...]).astype(_BF)

    xa = _dot(hn, win_ref[:, 0:d])
    _causal_conv_tile(xa, xa_ext, wca_ref, bca_ref, s4)
    pca_ref[...] = xa[tt - CONV_A_HALO:tt, :]
    xcb = s4[...].astype(_BF)
    _block_diag(xcb, wra_ref, bra_ref, s0)
    _block_diag(xcb, wrx_ref, brx_ref, s1)
    _rglru_coeffs(s4, s0, s1, lam_ref, s2, s3)

    ri = lax.broadcasted_iota(jnp.int32, (SUBLANE, d), 0)

    h_last = hcar[...]
    for r0 in range(0, tt, SUBLANE):
        a = s2[r0:r0 + SUBLANE, :]
        u = s3[r0:r0 + SUBLANE, :]
        for s in (1, 2, 4):
            keep = ri >= s
            a_sh = jnp.where(keep, pltpu.roll(a, s, axis=0), 1.0)
            u_sh = jnp.where(keep, pltpu.roll(u, s, axis=0), 0.0)
            u = a * u_sh + u
            a = a * a_sh
        hh = a * h_last + u
        s0[r0:r0 + SUBLANE, :] = hh
        h_last = jnp.broadcast_to(hh[SUBLANE - 1:SUBLANE, :], (SUBLANE, d))
    hcar[...] = h_last
    ph_ref[...] = h_last[0:1, :]

    v = _dot(hn, win_ref[:, d:2 * d]) * _sigmoid(_dot(hn, win_ref[:, 2 * d:3 * d]))
    pcf_ref[...] = v[tt - CONF_HALO:tt, :]
    _causal_conv_tile(v, v_ext, wdw_ref, bdw_ref, s1)
    sl = _layernorm_silu(s1[...], gln_ref[...], bln_ref[...])
    yb = _dot(sl.astype(_BF), wpw_ref[...]) + bpw_ref[...]

    ga = _dot(hn, win_ref[:, 3 * d:4 * d])
    gb = _dot(hn, win_ref[:, 4 * d:5 * d])
    xn_b = _tail(x, s0[...], yb, ga, gb, wout_ref, gffn_ref, h_ref)
    _route_and_sort(xn_b, wrt_ref, br_ref, pos_ref, prob_ref, cnt_ref, xs_ref)


def _routing_outputs(n_tiles, tt, d, ne):
    slr = _sorted_rows(tt, ne)
    shapes = [
        jax.ShapeDtypeStruct((n_tiles * slr, d), _F32),
        jax.ShapeDtypeStruct((TOP_K, n_tiles * tt), _F32),
        jax.ShapeDtypeStruct((TOP_K, n_tiles * tt), _F32),
        jax.ShapeDtypeStruct((n_tiles * ne, LANE), _F32),
    ]
    blocks = [(slr, d), (TOP_K, tt), (TOP_K, tt), (ne, LANE)]
    return shapes, blocks


def _prompt_mixer(x, wts, tt):
    bsz, t, d = x.shape
    nt = t // tt
    ne = wts["wrt"].shape[0]
    w_args = [wts[n] for n in _W_NAMES]
    r_shapes, r_blocks = _routing_outputs(bsz * nt, tt, d, ne)
    tile_major = lambda blk: pl.BlockSpec(blk, (lambda b, j: (b * nt + j, 0)) if blk[0] != TOP_K
                                          else (lambda b, j: (0, b * nt + j)))
    out_shape = [jax.ShapeDtypeStruct((bsz, t, d), _F32)] + r_shapes + [
        jax.ShapeDtypeStruct((bsz, CONV_A_HALO, d), _F32),
        jax.ShapeDtypeStruct((bsz, 1, d), _F32),
        jax.ShapeDtypeStruct((bsz, CONF_HALO, d), _F32),
    ]
    out_specs = [pl.BlockSpec((None, tt, d), lambda b, j: (b, j, 0))] + [tile_major(blk) for blk in r_blocks] + [
        pl.BlockSpec((None, CONV_A_HALO, d), lambda b, j: (b, 0, 0)),
        pl.BlockSpec((None, 1, d), lambda b, j: (b, 0, 0)),
        pl.BlockSpec((None, CONF_HALO, d), lambda b, j: (b, 0, 0)),
    ]
    scratch = [
        pltpu.VMEM((d // LANE, tt + CONV_A_HALO, LANE), _F32),
        pltpu.VMEM((d // LANE, tt + CONF_HALO, LANE), _F32),
        pltpu.VMEM((tt, d), _F32), pltpu.VMEM((tt, d), _F32), pltpu.VMEM((tt, d), _F32),
        pltpu.VMEM((tt, d), _F32), pltpu.VMEM((tt, d), _F32),
        pltpu.VMEM((SUBLANE, d), _F32),
    ]
    return pl.pallas_call(
        _prompt_mixer_body,
        grid=(bsz, nt),
        in_specs=[pl.BlockSpec((None, tt, d), lambda b, j: (b, j, 0))] + [_const_spec(w.shape) for w in w_args],
        out_specs=out_specs,
        out_shape=out_shape,
        scratch_shapes=scratch,
        compiler_params=pltpu.CompilerParams(dimension_semantics=("arbitrary", "arbitrary"),
                                             vmem_limit_bytes=VMEM_LIMIT),
        name="prompt_mixer",
    )(x, *w_args)


def _sample_mixer_body(nb, x_ref, sca_ref, sh_ref, scf_ref,
                       gmix_ref, win_ref, wca_ref, bca_ref, wra_ref, bra_ref, wrx_ref, brx_ref,
                       lam_ref, wdw_ref, bdw_ref, gln_ref, bln_ref, wpw_ref, bpw_ref, wout_ref,
                       gffn_ref, wrt_ref, br_ref,
                       h_ref, xs_ref, pos_ref, prob_ref, cnt_ref, v_out_ref, sh_out_ref, xa_out_ref,
                       s0, s1, s2, s3, s4):
    n, d = x_ref.shape
    steps = n // nb
    x = x_ref[...]
    hn = _rms(x, gmix_ref[...]).astype(_BF)

    xa = _dot(hn, win_ref[:, 0:d])
    xa_out_ref[...] = xa
    kw = wca_ref.shape[0]

    def xcat_a(slab):
        if slab < kw - 1:
            return sca_ref[slab * nb:(slab + 1) * nb, :]
        s = slab - (kw - 1)
        return xa[s * nb:(s + 1) * nb, :]

    for t in range(steps):
        acc = bca_ref[...] + jnp.zeros((nb, d), _F32)
        for k in range(kw):
            acc = acc + wca_ref[k:k + 1, :] * xcat_a(t + k)
        s4[t * nb:(t + 1) * nb, :] = acc
    xcb = s4[...].astype(_BF)
    _block_diag(xcb, wra_ref, bra_ref, s0)
    _block_diag(xcb, wrx_ref, brx_ref, s1)
    _rglru_coeffs(s4, s0, s1, lam_ref, s2, s3)
    h = sh_ref[...]
    for t in range(steps):
        h = s2[t * nb:(t + 1) * nb, :] * h + s3[t * nb:(t + 1) * nb, :]
        s0[t * nb:(t + 1) * nb, :] = h
    sh_out_ref[...] = h

    v = _dot(hn, win_ref[:, d:2 * d]) * _sigmoid(_dot(hn, win_ref[:, 2 * d:3 * d]))
    v_out_ref[...] = v
    kc = wdw_ref.shape[0]

    def xcat_v(slab):
        if slab < kc - 1:
            return scf_ref[slab * nb:(slab + 1) * nb, :]
        s = slab - (kc - 1)
        return v[s * nb:(s + 1) * nb, :]

    for t in range(steps):
        acc = bdw_ref[...] + jnp.zeros((nb, d), _F32)
        for k in range(kc):
            acc = acc + wdw_ref[k:k + 1, :] * xcat_v(t + k)
        s1[t * nb:(t + 1) * nb, :] = acc
    sl = _layernorm_silu(s1[...], gln_ref[...], bln_ref[...])
    yb = _dot(sl.astype(_BF), wpw_ref[...]) + bpw_ref[...]

    ga = _dot(hn, win_ref[:, 3 * d:4 * d])
    gb = _dot(hn, win_ref[:, 4 * d:5 * d])
    xn_b = _tail(x, s0[...], yb, ga, gb, wout_ref, gffn_ref, h_ref)
    _route_and_sort(xn_b, wrt_ref, br_ref, pos_ref, prob_ref, cnt_ref, xs_ref)


def _sample_mixer(x_cm, sca_cm, sh, scf_cm, wts, cb, steps):
    n, d = x_cm.shape
    rows = cb * steps
    n_chunks = n // rows
    ne = wts["wrt"].shape[0]
    kw = wts["wca"].shape[0]
    kc = wts["wdw"].shape[0]
    w_args = [wts[n_] for n_ in _W_NAMES]
    chunk = lambda r: pl.BlockSpec((r, d), lambda i: (i, 0))
    r_shapes, r_blocks = _routing_outputs(n_chunks, rows, d, ne)
    tile_major = lambda blk: pl.BlockSpec(blk, (lambda i: (i, 0)) if blk[0] != TOP_K else (lambda i: (0, i)))
    conf_state = pl.BlockSpec(((kc - 1) * cb, d), lambda i: (i, 0), pipeline_mode=pl.Buffered(1))
    in_specs = [chunk(rows), chunk((kw - 1) * cb), chunk(cb), conf_state] + \
        [_const_spec(w.shape) for w in w_args]
    out_shape = [jax.ShapeDtypeStruct((n, d), _F32)] + r_shapes + [
        jax.ShapeDtypeStruct((n, d), _F32),
        jax.ShapeDtypeStruct(sh.shape, _F32),
        jax.ShapeDtypeStruct((n, d), _F32),
    ]
    out_specs = [chunk(rows)] + [tile_major(blk) for blk in r_blocks] + [chunk(rows), chunk(cb), chunk(rows)]
    scratch = [pltpu.VMEM((rows, d), _F32)] * 5
    return pl.pallas_call(
        functools.partial(_sample_mixer_body, cb),
        grid=(n_chunks,),
        in_specs=in_specs,
        out_specs=out_specs,
        out_shape=out_shape,
        scratch_shapes=scratch,
        compiler_params=pltpu.CompilerParams(dimension_semantics=("arbitrary",),
                                             vmem_limit_bytes=VMEM_LIMIT),
        name="sample_mixer",
    )(x_cm, sca_cm, sh, scf_cm, *w_args)


def _expert_ffn_body(ntp,
                     te_ref, nu_ref, valid_ref, ilo_ref, ihi_ref, psrc_ref, poff_ref, pn_ref, tail_ref,
                     xsp_ref, xss_ref, wg_ref, bg_ref, wu_ref, bu_ref, wd_ref, bd_ref,
                     ysp_ref, yss_ref,
                     lhs, obuf, zbuf, wg_b, wu_b, wd_b, gsem, ssem, zsem):
    j = pl.program_id(0)
    nu = nu_ref[0]
    nt_all = tail_ref.shape[0] // 2
    tm = lhs.shape[1]

    def for_pieces(step, fn):
        row0 = step * nt_all

        def run(group, first, last):
            def body(i, c):
                n = pn_ref[row0 + i]

                @pl.when(n > 0)
                def _():
                    fn(group, pl.multiple_of(psrc_ref[row0 + i], SEG_ALIGN),
                       pl.multiple_of(poff_ref[row0 + i], SEG_ALIGN), pl.multiple_of(n, SEG_ALIGN))

                return c

            lax.fori_loop(first, last, body, 0)

        run(0, ilo_ref[step], jnp.minimum(ihi_ref[step], ntp))
        run(1, jnp.maximum(ilo_ref[step], ntp), ihi_ref[step])

    def gather_start(step, slot):
        def fn(group, row, off, n):
            src = (xsp_ref, xss_ref)[group]
            pltpu.make_async_copy(src.at[pl.ds(row, n), :], lhs.at[slot, pl.ds(off, n), :], gsem.at[slot]).start()
        for_pieces(step, fn)

    def scatter_start(step, slot):
        def fn(group, row, off, n):
            dst = (ysp_ref, yss_ref)[group]
            pltpu.make_async_copy(obuf.at[slot, pl.ds(off, n), :], dst.at[pl.ds(row, n), :], ssem.at[slot]).start()
        for_pieces(step, fn)

    def gather_wait(step, slot):
        n = pl.multiple_of(valid_ref[step], SEG_ALIGN)
        pltpu.make_async_copy(xsp_ref.at[pl.ds(0, n), :], lhs.at[slot, pl.ds(0, n), :], gsem.at[slot]).wait()

    def scatter_wait(step, slot):
        n = pl.multiple_of(valid_ref[step], SEG_ALIGN)
        pltpu.make_async_copy(obuf.at[slot, pl.ds(0, n), :], ysp_ref.at[pl.ds(0, n), :], ssem.at[slot]).wait()

    def zero_tails(do_wait):
        def run(group, first, last):
            def one(i, c):
                row = pl.multiple_of(tail_ref[2 * i], SEG_ALIGN)
                n = pl.multiple_of(tail_ref[2 * i + 1], SEG_ALIGN)

                @pl.when(n > 0)
                def _():
                    dst = (ysp_ref, yss_ref)[group]
                    cp = pltpu.make_async_copy(zbuf.at[pl.ds(0, n), :], dst.at[pl.ds(row, n), :], zsem)
                    cp.wait() if do_wait else cp.start()

                return c

            lax.fori_loop(first, last, one, 0)

        run(0, 0, ntp)
        run(1, ntp, nt_all)

    @pl.when(j == 0)
    def _():
        lhs[...] = jnp.zeros_like(lhs)
        zbuf[...] = jnp.zeros_like(zbuf)
        zero_tails(False)
        zero_tails(True)
        gather_start(0, 0)

    @pl.when(j + 1 < nu)
    def _():
        gather_start(j + 1, (j + 1) % 2)

    slot = j % 2

    @pl.when((j >= 2) & (j - 2 < nu))
    def _():
        scatter_wait(j - 2, slot)

    def compute(rows):
        x = lhs[slot, 0:rows, :].astype(_BF)
        g = jnp.minimum(_dot(x, wg_b[...]) + bg_ref[...], SWIGLU_LIMIT)
        u = jnp.clip(_dot(x, wu_b[...]) + bu_ref[...], -SWIGLU_LIMIT, SWIGLU_LIMIT)
        hdn = (u + 1.0) * (g * _sigmoid(SWIGLU_ALPHA * g))
        obuf[slot, 0:rows, :] = _dot(hdn.astype(_BF), wd_b[...]) + bd_ref[...]

    @pl.when(j < nu)
    def _():
        gather_wait(j, slot)
        prev = te_ref[jnp.maximum(j - 1, 0)]

        @pl.when((j == 0) | (te_ref[j] != prev))
        def _():
            wg_b[...] = wg_ref[...].astype(_BF)
            wu_b[...] = wu_ref[...].astype(_BF)
            wd_b[...] = wd_ref[...].astype(_BF)

        for rows in range(ROW_QUANTUM, tm + 1, ROW_QUANTUM):
            @pl.when((valid_ref[j] > rows - ROW_QUANTUM) & (valid_ref[j] <= rows))
            def _(rows=rows):
                compute(rows)

        scatter_start(j, slot)


def _expert_ffn(meta, xs_p, xs_s, wg, bg, wu, bu, wd, bd, tm, slr, n_tiles):
    ne, d, de = wg.shape
    ntp = xs_p.shape[0] // slr
    n_meta = len(meta)
    wspec = lambda a, b_: pl.BlockSpec((None, a, b_), lambda j, te, *_: (te[j], 0, 0))
    hbm = pl.BlockSpec(memory_space=pl.ANY)
    tail_rows = ne * SEG_ALIGN
    return pl.pallas_call(
        functools.partial(_expert_ffn_body, ntp),
        grid_spec=pltpu.PrefetchScalarGridSpec(
            num_scalar_prefetch=n_meta,
            grid=(n_tiles,),
            in_specs=[hbm, hbm, wspec(d, de), wspec(1, de), wspec(d, de), wspec(1, de), wspec(de, d), wspec(1, d)],
            out_specs=[hbm, hbm],
            scratch_shapes=[
                pltpu.VMEM((2, tm, d), _F32), pltpu.VMEM((2, tm, d), _F32),
                pltpu.VMEM((tail_rows, d), _F32),
                pltpu.VMEM((d, de), _BF), pltpu.VMEM((d, de), _BF), pltpu.VMEM((de, d), _BF),
                pltpu.SemaphoreType.DMA((2,)), pltpu.SemaphoreType.DMA((2,)), pltpu.SemaphoreType.DMA(()),
            ],
        ),
        out_shape=[jax.ShapeDtypeStruct(xs_p.shape, _F32), jax.ShapeDtypeStruct(xs_s.shape, _F32)],
        compiler_params=pltpu.CompilerParams(dimension_semantics=("arbitrary",),
                                             vmem_limit_bytes=VMEM_LIMIT),
        name="expert_ffn",
    )(*meta, xs_p, xs_s, wg, bg, wu, bu, wd, bd)


def _combine_body(tt, slr, ys_ref, h_ref, pos_ref, prob_ref, gfin_ref, y_ref):
    ciota = lax.broadcasted_iota(jnp.int32, (tt, slr), 1).astype(_F32)
    for q in range(h_ref.shape[0] // tt):
        rows = slice(q * tt, (q + 1) * tt)
        w = jnp.zeros((tt, slr), _F32)
        for k in range(TOP_K):
            w = jnp.where(ciota == pos_ref[rows, k:k + 1], prob_ref[rows, k:k + 1], w)
        ys = ys_ref[q * slr:(q + 1) * slr, :].astype(_BF)
        out = h_ref[rows, :] + _dot(w.astype(_BF), ys)
        y_ref[rows, :] = _rms(out, gfin_ref[...])


def _combine(ys, h, pos_t, prob_t, g_final, tt, slr):
    n, d = h.shape
    per_step = COMBINE_TILES if (n // tt) % COMBINE_TILES == 0 else 1
    slr, tt_step = slr * per_step, tt * per_step
    return pl.pallas_call(
        functools.partial(_combine_body, tt, slr // per_step),
        grid=(n // tt_step,),
        in_specs=[pl.BlockSpec((slr, d), lambda i: (i, 0)),
                  pl.BlockSpec((tt_step, d), lambda i: (i, 0)),
                  pl.BlockSpec((tt_step, TOP_K), lambda i: (i, 0)),
                  pl.BlockSpec((tt_step, TOP_K), lambda i: (i, 0)),
                  pl.BlockSpec((1, d), lambda i: (0, 0))],
        out_specs=pl.BlockSpec((tt_step, d), lambda i: (i, 0)),
        out_shape=jax.ShapeDtypeStruct((n, d), _F32),
        compiler_params=pltpu.CompilerParams(dimension_semantics=("arbitrary",),
                                             vmem_limit_bytes=VMEM_LIMIT),
        name="combine",
    )(ys, h, pos_t, prob_t, g_final)


def _tile_sizes(seq, nb, steps):
    tt = min(256, seq)
    return tt, tt // steps, ROW_TILE


def _row_tile_tables(counts, tm, n_tiles, ntp, slr):
    nt_all, ne = counts.shape
    cp = (counts + (SEG_ALIGN - 1)) // SEG_ALIGN * SEG_ALIGN
    lo = jnp.cumsum(cp, axis=1) - cp
    base = jnp.cumsum(cp, axis=0) - cp
    used = jnp.sum(cp, axis=1)
    total = jnp.sum(cp, axis=0)
    tiles_e = (total + tm - 1) // tm
    tile_end = jnp.cumsum(tiles_e)
    first_tile = tile_end - tiles_e
    ids = jnp.arange(n_tiles, dtype=jnp.int32)
    te = jnp.minimum(jnp.sum((tile_end[None, :] <= ids[:, None]).astype(jnp.int32), axis=1), ne - 1)
    onehot = te[:, None] == jnp.arange(ne, dtype=jnp.int32)[None, :]
    pick = lambda v: jnp.sum(jnp.where(onehot, v[None, :], 0), axis=1)
    pick_row = lambda m: jnp.sum(jnp.where(onehot[:, :, None], m.T[None, :, :], 0), axis=1)
    r0 = (ids - pick(first_tile)) * tm
    valid = jnp.clip(pick(total) - r0, 0, tm)
    seg_start = pick_row(base)
    seg_end = seg_start + pick_row(cp)
    r1 = r0 + valid
    ilo = jnp.sum((seg_end <= r0[:, None]).astype(jnp.int32), axis=1)
    ihi = jnp.sum((seg_start < r1[:, None]).astype(jnp.int32), axis=1)
    first = jnp.maximum(seg_start, r0[:, None])
    p_n = jnp.maximum(jnp.minimum(seg_end, r1[:, None]) - first, 0)
    tile_row0 = jnp.where(jnp.arange(nt_all) < ntp, jnp.arange(nt_all), jnp.arange(nt_all) - ntp) * slr
    p_src = tile_row0[None, :] + pick_row(lo) + (first - seg_start)
    p_off = first - r0[:, None]
    tails = jnp.stack([tile_row0 + used, slr - used], axis=1)
    i32 = lambda a: a.astype(jnp.int32).reshape(-1)
    return [i32(te), i32(tile_end[-1:]), i32(valid), i32(ilo), i32(ihi), i32(p_src), i32(p_off), i32(p_n),
            i32(tails)]


def kernel(x_prompt, x_sample, state_conv_a, state_h, state_conf, g_mix, w_in, w_conv_a, b_conv_a,
           w_rg_a, b_rg_a, w_rg_x, b_rg_x, rg_lambda, w_conf_dw, b_conf_dw, g_conf_ln, b_conf_ln,
           w_conf_pw, b_conf_pw, w_out, g_ffn, w_router, b_router, w_gate, b_gate, w_up, b_up,
           w_down, b_down, g_final):
    depth = g_mix.shape[0]
    assert depth == 1
    bsz, seq, d = x_prompt.shape
    nb, steps, _ = x_sample.shape
    ne = w_router.shape[-1]
    kw = w_conv_a.shape[1]
    kc = w_conf_dw.shape[1]
    n_p = bsz * seq
    n_s = nb * steps
    tt, cb, tm = _tile_sizes(seq, nb, steps)
    assert seq % tt == 0 and nb % cb == 0 and cb * steps == tt
    slr = _sorted_rows(tt, ne)
    ntp, nts = n_p // tt, n_s // tt

    row = lambda a: a[0].reshape(1, -1)
    wts = dict(
        gmix=row(g_mix), win=w_in[0].astype(_BF), wca=w_conv_a[0], bca=row(b_conv_a),
        wra=w_rg_a[0].astype(_BF), bra=row(b_rg_a), wrx=w_rg_x[0].astype(_BF), brx=row(b_rg_x),
        lam=row(rg_lambda), wdw=w_conf_dw[0], bdw=row(b_conf_dw), gln=row(g_conf_ln), bln=row(b_conf_ln),
        wpw=w_conf_pw[0].astype(_BF), bpw=row(b_conf_pw), wout=w_out[0].astype(_BF), gffn=row(g_ffn),
        wrt=w_router[0].T.astype(_BF), br=b_router[0].reshape(ne, 1),
    )

    (h_p, xs_p, pos_p, prob_p, cnt_p, pca, ph, pcf) = _prompt_mixer(x_prompt, wts, tt)
    to_cm = lambda a: jnp.swapaxes(a.reshape(nb // cb, cb, a.shape[1], d), 1, 2).reshape(-1, d)
    (h_s, xs_s, pos_s, prob_s, cnt_s, v_s, sh_new, xa_s) = _sample_mixer(
        to_cm(x_sample), to_cm(state_conv_a[0]), state_h[0], to_cm(state_conf[0]), wts, cb, steps)

    counts = jnp.concatenate([cnt_p[:, 0].reshape(ntp, ne), cnt_s[:, 0].reshape(nts, ne)], axis=0)
    n_tiles = ((ntp + nts) * slr) // tm + ne + 2
    meta = _row_tile_tables(counts.astype(jnp.int32), tm, n_tiles, ntp, slr)

    ys_p, ys_s = _expert_ffn(meta, xs_p, xs_s, w_gate[0], b_gate[0].reshape(ne, 1, -1), w_up[0],
                             b_up[0].reshape(ne, 1, -1), w_down[0], b_down[0].reshape(ne, 1, -1), tm, slr, n_tiles)
    gfin = g_final.reshape(1, d)
    y_p = _combine(ys_p, h_p.reshape(n_p, d), pos_p.T, prob_p.T, gfin, tt, slr)
    y_s = _combine(ys_s, h_s, pos_s.T, prob_s.T, gfin, tt, slr)

    nat = lambda a: jnp.swapaxes(a.reshape(nb // cb, steps, cb, d), 1, 2).reshape(nb, steps, d)
    y_prompt = y_p.reshape(bsz, seq, d)
    y_sample = nat(y_s)
    p_conv_a = pca[:, CONV_A_HALO - (kw - 1):, :][None]
    p_h = ph.reshape(1, bsz, d)
    p_conf = pcf[:, CONF_HALO - (kc - 1):, :][None]
    s_conv_a = jnp.concatenate([state_conv_a[0], nat(xa_s)], axis=1)[:, -(kw - 1):, :][None]
    s_h = sh_new[None]
    s_conf = jnp.concatenate([state_conf[0], nat(v_s)], axis=1)[:, -(kc - 1):, :][None]
    return (y_prompt, y_sample, p_conv_a, p_h, p_conf, s_conv_a, s_h, s_conf)
```

```python
import functools

import jax
import jax.numpy as jnp
from jax import lax
from jax.experimental import pallas as pl
from jax.experimental.pallas import tpu as pltpu

EPS = 1e-6
RG_C = 8.0
SWIGLU_LIMIT = 7.0
SWIGLU_ALPHA = 1.702
TOP_K = 4
LANE = 128
SUBLANE = 8
CONV_A_HALO = 8
CONF_HALO = 32
TAP_GROUP = 8
COMBINE_TILES = 2
ROW_QUANTUM = 256
SEG_ALIGN = SUBLANE
VMEM_LIMIT = 61 * 1024 * 1024

_BF = jnp.bfloat16
_F32 = jnp.float32


def _const_spec(shape):
    nd = len(shape)
    return pl.BlockSpec(shape, lambda *_: (0,) * nd, pipeline_mode=pl.Buffered(1))


def _sorted_rows(tt, ne):
    return TOP_K * tt + ne * SEG_ALIGN


def _rms(x, g):
    return x * lax.rsqrt(jnp.mean(x * x, axis=-1, keepdims=True) + EPS) * g


def _sigmoid(x):
    return jax.nn.sigmoid(x)


def _dot(a, b):
    return jnp.dot(a, b, preferred_element_type=_F32)


def _block_diag(xb, w_ref, b_ref, out_ref):
    nb, gb, _ = w_ref.shape
    for n in range(nb):
        cs = slice(n * gb, (n + 1) * gb)
        out_ref[:, cs] = _dot(xb[:, cs], w_ref[n]) + b_ref[:, cs]


def _rglru_coeffs(xc_ref, r_ref, i_ref, lam_ref, a_ref, u_ref):
    sp = jax.nn.softplus(-lam_ref[...])
    r = _sigmoid(r_ref[...])
    i = _sigmoid(i_ref[...])
    log_a = -RG_C * r * sp
    a = jnp.exp(log_a)
    a_ref[...] = a
    u_ref[...] = jnp.sqrt(-jnp.tanh(log_a) * (a * a + 1.0)) * (i * xc_ref[...])


def _layernorm_silu(vc, g, b):
    mu = jnp.mean(vc, axis=-1, keepdims=True)
    xc = vc - mu
    y = xc * lax.rsqrt(jnp.mean(xc * xc, axis=-1, keepdims=True) + EPS) * g + b
    return y * _sigmoid(y)


def _route_and_sort(xn_b, wrt_ref, br_ref, pos_ref, prob_ref, cnt_ref, xs_ref):
    ne = wrt_ref.shape[0]
    tt = xn_b.shape[0]
    slr = xs_ref.shape[0]
    logits = lax.dot_general(wrt_ref[...], xn_b, (((1,), (1,)), ((), ())),
                             preferred_element_type=_F32) + br_ref[...]
    eidx = lax.broadcasted_iota(jnp.int32, (ne, tt), 0).astype(_F32)
    work = logits
    vals, onehots = [], []
    for _ in range(TOP_K):
        m = jnp.max(work, axis=0, keepdims=True)
        pick = jnp.min(jnp.where(work == m, eidx, float(ne)), axis=0, keepdims=True)
        one = eidx == pick
        work = jnp.where(one, -jnp.inf, work)
        vals.append(m)
        onehots.append(one)
    exps = [jnp.exp(v - vals[0]) for v in vals]
    inv = 1.0 / (exps[0] + exps[1] + exps[2] + exps[3])
    sel = jnp.zeros((ne, tt), _F32)
    for one in onehots:
        sel = sel + one.astype(_F32)
    row = lax.broadcasted_iota(jnp.int32, (tt, tt), 0)
    col = lax.broadcasted_iota(jnp.int32, (tt, tt), 1)
    incl = _dot(sel.astype(_BF), (row <= col).astype(_BF))
    count = incl[:, tt - 1:tt]
    groups = jnp.floor((count + (SEG_ALIGN - 1)) * (1.0 / SEG_ALIGN))
    er = lax.broadcasted_iota(jnp.int32, (ne, ne), 0)
    ec = lax.broadcasted_iota(jnp.int32, (ne, ne), 1)
    lo = _dot((ec < er).astype(_BF), jnp.broadcast_to(groups, (ne, LANE)).astype(_BF))[:, 0:1] * SEG_ALIGN
    before = lo + (incl - sel)
    riota = lax.broadcasted_iota(jnp.int32, (slr, tt), 0).astype(_F32)
    perm = jnp.zeros((slr, tt), _F32)
    for k in range(TOP_K):
        pos = jnp.sum(jnp.where(onehots[k], before, 0.0), axis=0, keepdims=True)
        pos_ref[k:k + 1, :] = pos
        prob_ref[k:k + 1, :] = exps[k] * inv
        perm = jnp.where(riota == pos, 1.0, perm)
    cnt_ref[...] = jnp.broadcast_to(count, cnt_ref.shape)
    xs_ref[...] = _dot(perm.astype(_BF), xn_b)


def _tail(x, ya, yb, ga, gb, wout_ref, gffn_ref, h_ref):
    m = _sigmoid(ga) * ya + _sigmoid(gb) * yb
    h = x + _dot(m.astype(_BF), wout_ref[...])
    h_ref[...] = h
    return _rms(h, gffn_ref[...]).astype(_BF)


_W_NAMES = ["gmix", "win", "wca", "bca", "wra", "bra", "wrx", "brx", "lam", "wdw", "bdw", "gln", "bln",
            "wpw", "bpw", "wout", "gffn", "wrt", "br"]


def _causal_conv_tile(x, ext_ref, w_ref, b_ref, out_ref):
    tt, d = x.shape
    halo = ext_ref.shape[1] - tt
    kk = w_ref.shape[0]
    off = halo - (kk - 1)
    for c in range(d // LANE):
        cs = slice(c * LANE, (c + 1) * LANE)
        ext_ref[c, halo:halo + tt, :] = x[:, cs]
        bb = jnp.broadcast_to(b_ref[0:1, cs], (SUBLANE, LANE))
        for k0 in range(0, kk, TAP_GROUP):
            taps = range(k0, min(k0 + TAP_GROUP, kk))
            wb = {k: jnp.broadcast_to(w_ref[k:k + 1, cs], (SUBLANE, LANE)) for k in taps}
            for r0 in range(0, tt, SUBLANE):
                acc = bb if k0 == 0 else out_ref[r0:r0 + SUBLANE, cs]
                for k in taps:
                    acc = acc + wb[k] * ext_ref[c, r0 + off + k:r0 + off + k + SUBLANE, :]
                out_ref[r0:r0 + SUBLANE, cs] = acc
        ext_ref[c, 0:halo, :] = ext_ref[c, tt:tt + halo, :]


def _prompt_mixer_body(x_ref, gmix_ref, win_ref, wca_ref, bca_ref, wra_ref, bra_ref, wrx_ref, brx_ref,
                       lam_ref, wdw_ref, bdw_ref, gln_ref, bln_ref, wpw_ref, bpw_ref, wout_ref,
                       gffn_ref, wrt_ref, br_ref,
                       h_ref, xs_ref, pos_ref, prob_ref, cnt_ref, pca_ref, ph_ref, pcf_ref,
                       xa_ext, v_ext, s0, s1, s2, s3, s4, hcar):
    tt, d = x_ref.shape
    j = pl.program_id(1)

    @pl.when(j == 0)
    def _():
        hcar[...] = jnp.zeros_like(hcar)
        xa_ext[:, 0:CONV_A_HALO, :] = jnp.zeros((d // LANE, CONV_A_HALO, LANE), _F32)
        v_ext[:, 0:CONF_HALO, :] = jnp.zeros((d // LANE, CONF_HALO, LANE), _F32)

    x = x_ref[...]
    hn = _rms(x, gmix_ref[...]).astype(_BF)

    xa = _dot(hn, win_ref[:, 0:d])
    _causal_conv_tile(xa, xa_ext, wca_ref, bca_ref, s4)
    pca_ref[...] = xa[tt - CONV_A_HALO:tt, :]
    xcb = s4[...].astype(_BF)
    _block_diag(xcb, wra_ref, bra_ref, s0)
    _block_diag(xcb, wrx_ref, brx_ref, s1)
    _rglru_coeffs(s4, s0, s1, lam_ref, s2, s3)

    ri = lax.broadcasted_iota(jnp.int32, (SUBLANE, d), 0)

    h_last = hcar[...]
    for r0 in range(0, tt, SUBLANE):
        a = s2[r0:r0 + SUBLANE, :]
        u = s3[r0:r0 + SUBLANE, :]
        for s in (1, 2, 4):
            keep = ri >= s
            a_sh = jnp.where(keep, pltpu.roll(a, s, axis=0), 1.0)
            u_sh = jnp.where(keep, pltpu.roll(u, s, axis=0), 0.0)
            u = a * u_sh + u
            a = a * a_sh
        hh = a * h_last + u
        s0[r0:r0 + SUBLANE, :] = hh
        h_last = jnp.broadcast_to(hh[SUBLANE - 1:SUBLANE, :], (SUBLANE, d))
    hcar[...] = h_last
    ph_ref[...] = h_last[0:1, :]

    v = _dot(hn, win_ref[:, d:2 * d]) * _sigmoid(_dot(hn, win_ref[:, 2 * d:3 * d]))
    pcf_ref[...] = v[tt - CONF_HALO:tt, :]
    _causal_conv_tile(v, v_ext, wdw_ref, bdw_ref, s1)
    sl = _layernorm_silu(s1[...], gln_ref[...], bln_ref[...])
    yb = _dot(sl.astype(_BF), wpw_ref[...]) + bpw_ref[...]

    ga = _dot(hn, win_ref[:, 3 * d:4 * d])
    gb = _dot(hn, win_ref[:, 4 * d:5 * d])
    xn_b = _tail(x, s0[...], yb, ga, gb, wout_ref, gffn_ref, h_ref)
    _route_and_sort(xn_b, wrt_ref, br_ref, pos_ref, prob_ref, cnt_ref, xs_ref)


def _routing_outputs(n_tiles, tt, d, ne):
    slr = _sorted_rows(tt, ne)
    shapes = [
        jax.ShapeDtypeStruct((n_tiles * slr, d), _F32),
        jax.ShapeDtypeStruct((TOP_K, n_tiles * tt), _F32),
        jax.ShapeDtypeStruct((TOP_K, n_tiles * tt), _F32),
        jax.ShapeDtypeStruct((n_tiles * ne, LANE), _F32),
    ]
    blocks = [(slr, d), (TOP_K, tt), (TOP_K, tt), (ne, LANE)]
    return shapes, blocks


def _prompt_mixer(x, wts, tt):
    bsz, t, d = x.shape
    nt = t // tt
    ne = wts["wrt"].shape[0]
    w_args = [wts[n] for n in _W_NAMES]
    r_shapes, r_blocks = _routing_outputs(bsz * nt, tt, d, ne)
    tile_major = lambda blk: pl.BlockSpec(blk, (lambda b, j: (b * nt + j, 0)) if blk[0] != TOP_K
                                          else (lambda b, j: (0, b * nt + j)))
    out_shape = [jax.ShapeDtypeStruct((bsz, t, d), _F32)] + r_shapes + [
        jax.ShapeDtypeStruct((bsz, CONV_A_HALO, d), _F32),
        jax.ShapeDtypeStruct((bsz, 1, d), _F32),
        jax.ShapeDtypeStruct((bsz, CONF_HALO, d), _F32),
    ]
    out_specs = [pl.BlockSpec((None, tt, d), lambda b, j: (b, j, 0))] + [tile_major(blk) for blk in r_blocks] + [
        pl.BlockSpec((None, CONV_A_HALO, d), lambda b, j: (b, 0, 0)),
        pl.BlockSpec((None, 1, d), lambda b, j: (b, 0, 0)),
        pl.BlockSpec((None, CONF_HALO, d), lambda b, j: (b, 0, 0)),
    ]
    scratch = [
        pltpu.VMEM((d // LANE, tt + CONV_A_HALO, LANE), _F32),
        pltpu.VMEM((d // LANE, tt + CONF_HALO, LANE), _F32),
        pltpu.VMEM((tt, d), _F32), pltpu.VMEM((tt, d), _F32), pltpu.VMEM((tt, d), _F32),
        pltpu.VMEM((tt, d), _F32), pltpu.VMEM((tt, d), _F32),
        pltpu.VMEM((SUBLANE, d), _F32),
    ]
    return pl.pallas_call(
        _prompt_mixer_body,
        grid=(bsz, nt),
        in_specs=[pl.BlockSpec((None, tt, d), lambda b, j: (b, j, 0))] + [_const_spec(w.shape) for w in w_args],
        out_specs=out_specs,
        out_shape=out_shape,
        scratch_shapes=scratch,
        compiler_params=pltpu.CompilerParams(dimension_semantics=("arbitrary", "arbitrary"),
                                             vmem_limit_bytes=VMEM_LIMIT),
        name="prompt_mixer",
    )(x, *w_args)


def _sample_mixer_body(nb, x_ref, sca_ref, sh_ref, scf_ref,
                       gmix_ref, win_ref, wca_ref, bca_ref, wra_ref, bra_ref, wrx_ref, brx_ref,
                       lam_ref, wdw_ref, bdw_ref, gln_ref, bln_ref, wpw_ref, bpw_ref, wout_ref,
                       gffn_ref, wrt_ref, br_ref,
                       h_ref, xs_ref, pos_ref, prob_ref, cnt_ref, v_out_ref, sh_out_ref, xa_out_ref,
                       s0, s1, s2, s3, s4):
    n, d = x_ref.shape
    steps = n // nb
    x = x_ref[...]
    hn = _rms(x, gmix_ref[...]).astype(_BF)

    xa = _dot(hn, win_ref[:, 0:d])
    xa_out_ref[...] = xa
    kw = wca_ref.shape[0]

    def xcat_a(slab):
        if slab < kw - 1:
            return sca_ref[slab * nb:(slab + 1) * nb, :]
        s = slab - (kw - 1)
        return xa[s * nb:(s + 1) * nb, :]

    for t in range(steps):
        acc = bca_ref[...] + jnp.zeros((nb, d), _F32)
        for k in range(kw):
            acc = acc + wca_ref[k:k + 1, :] * xcat_a(t + k)
        s4[t * nb:(t + 1) * nb, :] = acc
    xcb = s4[...].astype(_BF)
    _block_diag(xcb, wra_ref, bra_ref, s0)
    _block_diag(xcb, wrx_ref, brx_ref, s1)
    _rglru_coeffs(s4, s0, s1, lam_ref, s2, s3)
    h = sh_ref[...]
    for t in range(steps):
        h = s2[t * nb:(t + 1) * nb, :] * h + s3[t * nb:(t + 1) * nb, :]
        s0[t * nb:(t + 1) * nb, :] = h
    sh_out_ref[...] = h

    v = _dot(hn, win_ref[:, d:2 * d]) * _sigmoid(_dot(hn, win_ref[:, 2 * d:3 * d]))
    v_out_ref[...] = v
    kc = wdw_ref.shape[0]

    def xcat_v(slab):
        if slab < kc - 1:
            return scf_ref[slab * nb:(slab + 1) * nb, :]
        s = slab - (kc - 1)
        return v[s * nb:(s + 1) * nb, :]

    for t in range(steps):
        acc = bdw_ref[...] + jnp.zeros((nb, d), _F32)
        for k in range(kc):
            acc = acc + wdw_ref[k:k + 1, :] * xcat_v(t + k)
        s1[t * nb:(t + 1) * nb, :] = acc
    sl = _layernorm_silu(s1[...], gln_ref[...], bln_ref[...])
    yb = _dot(sl.astype(_BF), wpw_ref[...]) + bpw_ref[...]

    ga = _dot(hn, win_ref[:, 3 * d:4 * d])
    gb = _dot(hn, win_ref[:, 4 * d:5 * d])
    xn_b = _tail(x, s0[...], yb, ga, gb, wout_ref, gffn_ref, h_ref)
    _route_and_sort(xn_b, wrt_ref, br_ref, pos_ref, prob_ref, cnt_ref, xs_ref)


def _sample_mixer(x_cm, sca_cm, sh, scf_cm, wts, cb, steps):
    n, d = x_cm.shape
    rows = cb * steps
    n_chunks = n // rows
    ne = wts["wrt"].shape[0]
    kw = wts["wca"].shape[0]
    kc = wts["wdw"].shape[0]
    w_args = [wts[n_] for n_ in _W_NAMES]
    chunk = lambda r: pl.BlockSpec((r, d), lambda i: (i, 0))
    r_shapes, r_blocks = _routing_outputs(n_chunks, rows, d, ne)
    tile_major = lambda blk: pl.BlockSpec(blk, (lambda i: (i, 0)) if blk[0] != TOP_K else (lambda i: (0, i)))
    conf_state = pl.BlockSpec(((kc - 1) * cb, d), lambda i: (i, 0), pipeline_mode=pl.Buffered(1))
    in_specs = [chunk(rows), chunk((kw - 1) * cb), chunk(cb), conf_state] + \
        [_const_spec(w.shape) for w in w_args]
    out_shape = [jax.ShapeDtypeStruct((n, d), _F32)] + r_shapes + [
        jax.ShapeDtypeStruct((n, d), _F32),
        jax.ShapeDtypeStruct(sh.shape, _F32),
        jax.ShapeDtypeStruct((n, d), _F32),
    ]
    out_specs = [chunk(rows)] + [tile_major(blk) for blk in r_blocks] + [chunk(rows), chunk(cb), chunk(rows)]
    scratch = [pltpu.VMEM((rows, d), _F32)] * 5
    return pl.pallas_call(
        functools.partial(_sample_mixer_body, cb),
        grid=(n_chunks,),
        in_specs=in_specs,
        out_specs=out_specs,
        out_shape=out_shape,
        scratch_shapes=scratch,
        compiler_params=pltpu.CompilerParams(dimension_semantics=("arbitrary",),
                                             vmem_limit_bytes=VMEM_LIMIT),
        name="sample_mixer",
    )(x_cm, sca_cm, sh, scf_cm, *w_args)


def _expert_ffn_body(ntp,
                     te_ref, nu_ref, valid_ref, ilo_ref, ihi_ref, psrc_ref, poff_ref, pn_ref, tail_ref,
                     xsp_ref, xss_ref, wg_ref, bg_ref, wu_ref, bu_ref, wd_ref, bd_ref,
                     ysp_ref, yss_ref,
                     lhs, obuf, zbuf, wg_b, wu_b, wd_b, gsem, ssem, zsem):
    j = pl.program_id(0)
    nu = nu_ref[0]
    nt_all = tail_ref.shape[0] // 2
    tm = lhs.shape[1]

    def for_pieces(step, fn):
        row0 = step * nt_all

        def run(group, first, last):
            def body(i, c):
                n = pn_ref[row0 + i]

                @pl.when(n > 0)
                def _():
                    fn(group, pl.multiple_of(psrc_ref[row0 + i], SEG_ALIGN),
                       pl.multiple_of(poff_ref[row0 + i], SEG_ALIGN), pl.multiple_of(n, SEG_ALIGN))

                return c

            lax.fori_loop(first, last, body, 0)

        run(0, ilo_ref[step], jnp.minimum(ihi_ref[step], ntp))
        run(1, jnp.maximum(ilo_ref[step], ntp), ihi_ref[step])

    def gather_start(step, slot):
        def fn(group, row, off, n):
            src = (xsp_ref, xss_ref)[group]
            pltpu.make_async_copy(src.at[pl.ds(row, n), :], lhs.at[slot, pl.ds(off, n), :], gsem.at[slot]).start()
        for_pieces(step, fn)

    def scatter_start(step, slot):
        def fn(group, row, off, n):
            dst = (ysp_ref, yss_ref)[group]
            pltpu.make_async_copy(obuf.at[slot, pl.ds(off, n), :], dst.at[pl.ds(row, n), :], ssem.at[slot]).start()
        for_pieces(step, fn)

    def gather_wait(step, slot):
        n = pl.multiple_of(valid_ref[step], SEG_ALIGN)
        pltpu.make_async_copy(xsp_ref.at[pl.ds(0, n), :], lhs.at[slot, pl.ds(0, n), :], gsem.at[slot]).wait()

    def scatter_wait(step, slot):
        n = pl.multiple_of(valid_ref[step], SEG_ALIGN)
        pltpu.make_async_copy(obuf.at[slot, pl.ds(0, n), :], ysp_ref.at[pl.ds(0, n), :], ssem.at[slot]).wait()

    def zero_tails(do_wait):
        def run(group, first, last):
            def one(i, c):
                row = pl.multiple_of(tail_ref[2 * i], SEG_ALIGN)
                n = pl.multiple_of(tail_ref[2 * i + 1], SEG_ALIGN)

                @pl.when(n > 0)
                def _():
                    dst = (ysp_ref, yss_ref)[group]
                    cp = pltpu.make_async_copy(zbuf.at[pl.ds(0, n), :], dst.at[pl.ds(row, n), :], zsem)
                    cp.wait() if do_wait else cp.start()

                return c

            lax.fori_loop(first, last, one, 0)

        run(0, 0, ntp)
        run(1, ntp, nt_all)

    @pl.when(j == 0)
    def _():
        lhs[...] = jnp.zeros_like(lhs)
        zbuf[...] = jnp.zeros_like(zbuf)
        zero_tails(False)
        zero_tails(True)
        gather_start(0, 0)

    @pl.when(j + 1 < nu)
    def _():
        gather_start(j + 1, (j + 1) % 2)

    slot = j % 2

    @pl.when((j >= 2) & (j - 2 < nu))
    def _():
        scatter_wait(j - 2, slot)

    def compute(rows):
        x = lhs[slot, 0:rows, :].astype(_BF)
        g = jnp.minimum(_dot(x, wg_b[...]) + bg_ref[...], SWIGLU_LIMIT)
        u = jnp.clip(_dot(x, wu_b[...]) + bu_ref[...], -SWIGLU_LIMIT, SWIGLU_LIMIT)
        hdn = (u + 1.0) * (g * _sigmoid(SWIGLU_ALPHA * g))
        obuf[slot, 0:rows, :] = _dot(hdn.astype(_BF), wd_b[...]) + bd_ref[...]

    @pl.when(j < nu)
    def _():
        gather_wait(j, slot)
        prev = te_ref[jnp.maximum(j - 1, 0)]

        @pl.when((j == 0) | (te_ref[j] != prev))
        def _():
            wg_b[...] = wg_ref[...].astype(_BF)
            wu_b[...] = wu_ref[...].astype(_BF)
            wd_b[...] = wd_ref[...].astype(_BF)

        for rows in range(ROW_QUANTUM, tm + 1, ROW_QUANTUM):
            @pl.when((valid_ref[j] > rows - ROW_QUANTUM) & (valid_ref[j] <= rows))
            def _(rows=rows):
                compute(rows)

        scatter_start(j, slot)


def _expert_ffn(meta, xs_p, xs_s, wg, bg, wu, bu, wd, bd, tm, slr, n_tiles):
    ne, d, de = wg.shape
    ntp = xs_p.shape[0] // slr
    n_meta = len(meta)
    wspec = lambda a, b_: pl.BlockSpec((None, a, b_), lambda j, te, *_: (te[j], 0, 0))
    hbm = pl.BlockSpec(memory_space=pl.ANY)
    tail_rows = ne * SEG_ALIGN
    return pl.pallas_call(
        functools.partial(_expert_ffn_body, ntp),
        grid_spec=pltpu.PrefetchScalarGridSpec(
            num_scalar_prefetch=n_meta,
            grid=(n_tiles,),
            in_specs=[hbm, hbm, wspec(d, de), wspec(1, de), wspec(d, de), wspec(1, de), wspec(de, d), wspec(1, d)],
            out_specs=[hbm, hbm],
            scratch_shapes=[
                pltpu.VMEM((2, tm, d), _F32), pltpu.VMEM((2, tm, d), _F32),
                pltpu.VMEM((tail_rows, d), _F32),
                pltpu.VMEM((d, de), _BF), pltpu.VMEM((d, de), _BF), pltpu.VMEM((de, d), _BF),
                pltpu.SemaphoreType.DMA((2,)), pltpu.SemaphoreType.DMA((2,)), pltpu.SemaphoreType.DMA(()),
            ],
        ),
        out_shape=[jax.ShapeDtypeStruct(xs_p.shape, _F32), jax.ShapeDtypeStruct(xs_s.shape, _F32)],
        compiler_params=pltpu.CompilerParams(dimension_semantics=("arbitrary",),
                                             vmem_limit_bytes=VMEM_LIMIT),
        name="expert_ffn",
    )(*meta, xs_p, xs_s, wg, bg, wu, bu, wd, bd)


def _combine_body(tt, slr, ys_ref, h_ref, pos_ref, prob_ref, gfin_ref, y_ref):
    ciota = lax.broadcasted_iota(jnp.int32, (tt, slr), 1).astype(_F32)
    for q in range(h_ref.shape[0] // tt):
        rows = slice(q * tt, (q + 1) * tt)
        w = jnp.zeros((tt, slr), _F32)
        for k in range(TOP_K):
            w = jnp.where(ciota == pos_ref[rows, k:k + 1], prob_ref[rows, k:k + 1], w)
        ys = ys_ref[q * slr:(q + 1) * slr, :].astype(_BF)
        out = h_ref[rows, :] + _dot(w.astype(_BF), ys)
        y_ref[rows, :] = _rms(out, gfin_ref[...])


def _combine(ys, h, pos_t, prob_t, g_final, tt, slr):
    n, d = h.shape
    per_step = COMBINE_TILES if (n // tt) % COMBINE_TILES == 0 else 1
    slr, tt_step = slr * per_step, tt * per_step
    return pl.pallas_call(
        functools.partial(_combine_body, tt, slr // per_step),
        grid=(n // tt_step,),
        in_specs=[pl.BlockSpec((slr, d), lambda i: (i, 0)),
                  pl.BlockSpec((tt_step, d), lambda i: (i, 0)),
                  pl.BlockSpec((tt_step, TOP_K), lambda i: (i, 0)),
                  pl.BlockSpec((tt_step, TOP_K), lambda i: (i, 0)),
                  pl.BlockSpec((1, d), lambda i: (0, 0))],
        out_specs=pl.BlockSpec((tt_step, d), lambda i: (i, 0)),
        out_shape=jax.ShapeDtypeStruct((n, d), _F32),
        compiler_params=pltpu.CompilerParams(dimension_semantics=("arbitrary",),
                                             vmem_limit_bytes=VMEM_LIMIT),
        name="combine",
    )(ys, h, pos_t, prob_t, g_final)


def _tile_sizes(seq, nb, steps):
    tt = min(256, seq)
    return tt, tt // steps, 5 * ROW_QUANTUM


def _row_tile_tables(counts, tm, n_tiles, ntp, slr):
    nt_all, ne = counts.shape
    cp = (counts + (SEG_ALIGN - 1)) // SEG_ALIGN * SEG_ALIGN
    lo = jnp.cumsum(cp, axis=1) - cp
    base = jnp.cumsum(cp, axis=0) - cp
    used = jnp.sum(cp, axis=1)
    total = jnp.sum(cp, axis=0)
    tiles_e = (total + tm - 1) // tm
    tile_end = jnp.cumsum(tiles_e)
    first_tile = tile_end - tiles_e
    ids = jnp.arange(n_tiles, dtype=jnp.int32)
    te = jnp.minimum(jnp.sum((tile_end[None, :] <= ids[:, None]).astype(jnp.int32), axis=1), ne - 1)
    onehot = te[:, None] == jnp.arange(ne, dtype=jnp.int32)[None, :]
    pick = lambda v: jnp.sum(jnp.where(onehot, v[None, :], 0), axis=1)
    pick_row = lambda m: jnp.sum(jnp.where(onehot[:, :, None], m.T[None, :, :], 0), axis=1)
    r0 = (ids - pick(first_tile)) * tm
    valid = jnp.clip(pick(total) - r0, 0, tm)
    seg_start = pick_row(base)
    seg_end = seg_start + pick_row(cp)
    r1 = r0 + valid
    ilo = jnp.sum((seg_end <= r0[:, None]).astype(jnp.int32), axis=1)
    ihi = jnp.sum((seg_start < r1[:, None]).astype(jnp.int32), axis=1)
    first = jnp.maximum(seg_start, r0[:, None])
    p_n = jnp.maximum(jnp.minimum(seg_end, r1[:, None]) - first, 0)
    tile_row0 = jnp.where(jnp.arange(nt_all) < ntp, jnp.arange(nt_all), jnp.arange(nt_all) - ntp) * slr
    p_src = tile_row0[None, :] + pick_row(lo) + (first - seg_start)
    p_off = first - r0[:, None]
    tails = jnp.stack([tile_row0 + used, slr - used], axis=1)
    i32 = lambda a: a.astype(jnp.int32).reshape(-1)
    return [i32(te), i32(tile_end[-1:]), i32(valid), i32(ilo), i32(ihi), i32(p_src), i32(p_off), i32(p_n),
            i32(tails)]


def kernel(x_prompt, x_sample, state_conv_a, state_h, state_conf, g_mix, w_in, w_conv_a, b_conv_a,
           w_rg_a, b_rg_a, w_rg_x, b_rg_x, rg_lambda, w_conf_dw, b_conf_dw, g_conf_ln, b_conf_ln,
           w_conf_pw, b_conf_pw, w_out, g_ffn, w_router, b_router, w_gate, b_gate, w_up, b_up,
           w_down, b_down, g_final):
    depth = g_mix.shape[0]
    assert depth == 1
    bsz, seq, d = x_prompt.shape
    nb, steps, _ = x_sample.shape
    ne = w_router.shape[-1]
    kw = w_conv_a.shape[1]
    kc = w_conf_dw.shape[1]
    n_p = bsz * seq
    n_s = nb * steps
    tt, cb, tm = _tile_sizes(seq, nb, steps)
    assert seq % tt == 0 and nb % cb == 0 and cb * steps == tt
    slr = _sorted_rows(tt, ne)
    ntp, nts = n_p // tt, n_s // tt

    row = lambda a: a[0].reshape(1, -1)
    wts = dict(
        gmix=row(g_mix), win=w_in[0].astype(_BF), wca=w_conv_a[0], bca=row(b_conv_a),
        wra=w_rg_a[0].astype(_BF), bra=row(b_rg_a), wrx=w_rg_x[0].astype(_BF), brx=row(b_rg_x),
        lam=row(rg_lambda), wdw=w_conf_dw[0], bdw=row(b_conf_dw), gln=row(g_conf_ln), bln=row(b_conf_ln),
        wpw=w_conf_pw[0].astype(_BF), bpw=row(b_conf_pw), wout=w_out[0].astype(_BF), gffn=row(g_ffn),
        wrt=w_router[0].T.astype(_BF), br=b_router[0].reshape(ne, 1),
    )

    (h_p, xs_p, pos_p, prob_p, cnt_p, pca, ph, pcf) = _prompt_mixer(x_prompt, wts, tt)
    to_cm = lambda a: jnp.swapaxes(a.reshape(nb // cb, cb, a.shape[1], d), 1, 2).reshape(-1, d)
    (h_s, xs_s, pos_s, prob_s, cnt_s, v_s, sh_new, xa_s) = _sample_mixer(
        to_cm(x_sample), to_cm(state_conv_a[0]), state_h[0], to_cm(state_conf[0]), wts, cb, steps)

    counts = jnp.concatenate([cnt_p[:, 0].reshape(ntp, ne), cnt_s[:, 0].reshape(nts, ne)], axis=0)
    n_tiles = ((ntp + nts) * slr) // tm + ne + 2
    meta = _row_tile_tables(counts.astype(jnp.int32), tm, n_tiles, ntp, slr)

    ys_p, ys_s = _expert_ffn(meta, xs_p, xs_s, w_gate[0], b_gate[0].reshape(ne, 1, -1), w_up[0],
                             b_up[0].reshape(ne, 1, -1), w_down[0], b_down[0].reshape(ne, 1, -1), tm, slr, n_tiles)
    gfin = g_final.reshape(1, d)
    y_p = _combine(ys_p, h_p.reshape(n_p, d), pos_p.T, prob_p.T, gfin, tt, slr)
    y_s = _combine(ys_s, h_s, pos_s.T, prob_s.T, gfin, tt, slr)

    nat = lambda a: jnp.swapaxes(a.reshape(nb // cb, steps, cb, d), 1, 2).reshape(nb, steps, d)
    y_prompt = y_p.reshape(bsz, seq, d)
    y_sample = nat(y_s)
    p_conv_a = pca[:, CONV_A_HALO - (kw - 1):, :][None]
    p_h = ph.reshape(1, bsz, d)
    p_conf = pcf[:, CONF_HALO - (kc - 1):, :][None]
    s_conv_a = jnp.concatenate([state_conv_a[0], nat(xa_s)], axis=1)[:, -(kw - 1):, :][None]
    s_h = sh_new[None]
    s_conf = jnp.concatenate([state_conf[0], nat(v_s)], axis=1)[:, -(kc - 1):, :][None]
    return (y_prompt, y_sample, p_conv_a, p_h, p_conf, s_conv_a, s_h, s_conf)
```
